```python
import math
import jax
import jax.numpy as jnp
from jax import lax
import numpy as np

D_MODEL = 4096
BATCH = 8
SEQ = 2048
DEPTH = 1
DEC_BATCH = 2
DEC_SEQ = 8192
PAST_LEN = 128

N_META = 16
CHUNK = 128
Q_BLOCK = 128
A_WIDTH = D_MODEL // 2
A_V_DIM = 256
A_HEADS = A_WIDTH // A_V_DIM
A_QK_DIM = A_V_DIM // 2
A_QK_WIDTH = 2 * A_HEADS * A_QK_DIM
A_ROT_DIM = A_QK_DIM // 4
ROPE_THETA = 500000.0
R_WIDTH = D_MODEL - A_WIDTH
R_HEADS = 8
R_V_DIM = R_WIDTH // R_HEADS
R_QK_DIM = R_V_DIM // 2
R_QK_WIDTH = R_HEADS * R_QK_DIM
RET_THETA = 10000.0
PROJ_SIZES = [A_QK_WIDTH, A_QK_WIDTH, A_WIDTH, A_WIDTH, R_QK_WIDTH, R_QK_WIDTH, R_WIDTH, R_WIDTH]
PROJ_OUT = sum(PROJ_SIZES)
SPLIT_POINTS = [int(v) for v in np.cumsum(PROJ_SIZES)[:-1]]
NORM_EPS = 1e-6
NEG_BIG = -1e30

kernel_name = "hybrid_diffattn_retention_encoder"


def rms_norm(x, gain):
    xf = x.astype(jnp.float32)
    y = xf * lax.rsqrt(jnp.mean(xf * xf, axis=-1, keepdims=True) + NORM_EPS)
    return (y * gain.astype(jnp.float32)).astype(x.dtype)


def rotary(x, pos, rot_dim, theta):
    half = rot_dim // 2
    inv = jnp.power(jnp.float32(theta), -jnp.arange(half, dtype=jnp.float32) * 2.0 / rot_dim)
    ang = pos.astype(jnp.float32)[:, None] * inv[None, :]
    cos = jnp.cos(ang)[None, :, None, :]
    sin = jnp.sin(ang)[None, :, None, :]
    xf = x.astype(jnp.float32)
    x1 = xf[..., :half]
    x2 = xf[..., half:rot_dim]
    out = jnp.concatenate([x1 * cos - x2 * sin, x2 * cos + x1 * sin, xf[..., rot_dim:]], axis=-1)
    return out.astype(x.dtype)


def diff_attention(q, k, v, key_bias, lam):
    B, P, H2, dh = q.shape
    H = H2 // 2
    nb = P // Q_BLOCK
    qb = q.reshape(B, nb, Q_BLOCK, H2, dh).transpose(1, 0, 2, 3, 4)
    scale = dh ** -0.5

    def one_block(q_blk):
        s = jnp.einsum('bqhd,bkhd->bhqk', q_blk, k, preferred_element_type=jnp.float32) * scale + key_bias
        a = jax.nn.softmax(s, axis=-1).reshape(B, H, 2, Q_BLOCK, P)
        w = (a[:, :, 0] - lam * a[:, :, 1]).astype(v.dtype)
        return jnp.einsum('bhqk,bkhe->bqhe', w, v)

    o = lax.map(one_block, qb)
    return o.transpose(1, 0, 2, 3, 4).reshape(B, P, H, v.shape[-1])


def retention_scan(q, k, v, log_gamma, inclusive):
    B, P, H, dk = q.shape
    dv = v.shape[-1]
    nc = P // CHUNK

    def chunks(t):
        return t.reshape(B, nc, CHUNK, H, t.shape[-1]).transpose(1, 0, 3, 2, 4)

    qc, kc, vc = chunks(q), chunks(k), chunks(v)
    idx = jnp.arange(CHUNK, dtype=jnp.float32)
    rel = idx[:, None] - idx[None, :]
    mask = (rel >= 0) if inclusive else (rel > 0)
    lg = log_gamma[:, None, None]
    d_intra = jnp.where(mask, jnp.exp(jnp.where(mask, rel, 0.0) * lg), 0.0)
    q_decay = jnp.exp((idx + 1.0)[None, :] * log_gamma[:, None])[None, :, :, None]
    k_decay = jnp.exp((CHUNK - 1.0 - idx)[None, :] * log_gamma[:, None])[None, :, :, None]
    chunk_decay = jnp.exp(CHUNK * log_gamma)[None, :, None, None]

    def step(state, xs):
        qi, ki, vi = xs
        s = jnp.einsum('bhid,bhjd->bhij', qi, ki) * d_intra
        intra = jnp.einsum('bhij,bhje->bhie', s, vi)
        inter = jnp.einsum('bhid,bhde->bhie', qi, state) * q_decay
        new_state = state * chunk_decay + jnp.einsum('bhjd,bhje->bhde', ki * k_decay, vi)
        return new_state, intra + inter

    s0 = jnp.zeros((B, H, dk, dv), jnp.float32)
    _, out = lax.scan(step, s0, (qc, kc, vc))
    return out.transpose(1, 0, 3, 2, 4).reshape(B, P, H, dv)


def bidirectional_retention(q, k, v, decay_fwd_logit, decay_bwd_logit):
    lg_f = jax.nn.log_sigmoid(decay_fwd_logit.astype(jnp.float32))
    lg_b = jax.nn.log_sigmoid(decay_bwd_logit.astype(jnp.float32))
    fwd = retention_scan(q, k, v, lg_f, True)
    bwd = retention_scan(q[:, ::-1], k[:, ::-1], v[:, ::-1], lg_b, False)[:, ::-1]
    return fwd + bwd


def encoder_layer(h, pos, valid, key_bias, lambda_init, norm_gain, w_in, lq1, lk1, lq2, lk2,
                  subln_gain, dec_f, dec_b, ret_gain, w_out):
    B, P, _ = h.shape
    u = rms_norm(h, norm_gain)
    proj = jnp.einsum('bpd,de->bpe', u, w_in)
    aq, ak, av, ag, rq, rk, rv, rg = jnp.split(proj, SPLIT_POINTS, axis=-1)

    aq = rotary(aq.reshape(B, P, 2 * A_HEADS, A_QK_DIM), pos, A_ROT_DIM, ROPE_THETA)
    ak = rotary(ak.reshape(B, P, 2 * A_HEADS, A_QK_DIM), pos, A_ROT_DIM, ROPE_THETA)
    av = av.reshape(B, P, A_HEADS, A_V_DIM)
    f32 = jnp.float32
    lam = (jnp.exp(jnp.sum(lq1.astype(f32) * lk1.astype(f32)))
           - jnp.exp(jnp.sum(lq2.astype(f32) * lk2.astype(f32))) + lambda_init)
    ao = diff_attention(aq, ak, av, key_bias, lam)
    ao = rms_norm(ao, subln_gain) * (1.0 - lambda_init)
    ao = ao.reshape(B, P, A_WIDTH).astype(h.dtype) * jax.nn.silu(ag)

    vmask = valid.astype(f32)[None, :, None, None]
    rq = rotary(rq.reshape(B, P, R_HEADS, R_QK_DIM), pos, R_QK_DIM, RET_THETA).astype(f32)
    rk = rotary(rk.reshape(B, P, R_HEADS, R_QK_DIM), pos, R_QK_DIM, RET_THETA).astype(f32) * (R_QK_DIM ** -0.5) * vmask
    rv = rv.reshape(B, P, R_HEADS, R_V_DIM).astype(f32)
    ro = bidirectional_retention(rq, rk, rv, dec_f, dec_b)
    ro = rms_norm(ro, ret_gain).astype(h.dtype).reshape(B, P, R_WIDTH) * jax.nn.silu(rg)

    mixed = jnp.concatenate([ao, ro], axis=-1)
    return h + jnp.einsum('bpe,ed->bpd', mixed, w_out)


def run_trunk(x, meta_tokens, norm_gain, w_in, lambda_q1, lambda_k1, lambda_q2, lambda_k2,
              attn_subln_gain, ret_decay_fwd, ret_decay_bwd, ret_norm_gain, w_out, final_norm_gain):
    B, S, D = x.shape
    pad = jnp.zeros((B, CHUNK - N_META, D), x.dtype)
    meta = jnp.broadcast_to(meta_tokens.astype(x.dtype)[None], (B, N_META, D))
    h = jnp.concatenate([pad, meta, x], axis=1)
    P = S + CHUNK
    pos = jnp.arange(P, dtype=jnp.int32) - (CHUNK - N_META)
    valid = pos >= 0
    key_bias = jnp.where(valid, 0.0, NEG_BIG).astype(jnp.float32)
    for l in range(DEPTH):
        lambda_init = 0.8 - 0.6 * math.exp(-0.3 * l)
        h = encoder_layer(h, pos, valid, key_bias, lambda_init, norm_gain[l], w_in[l],
                          lambda_q1[l], lambda_k1[l], lambda_q2[l], lambda_k2[l],
                          attn_subln_gain[l], ret_decay_fwd[l], ret_decay_bwd[l],
                          ret_norm_gain[l], w_out[l])
    h = rms_norm(h, final_norm_gain)
    return h[:, CHUNK:]


def setup_inputs(seed: int = 0) -> dict:
    key = jax.random.key(seed)
    ks = jax.random.split(key, 16)
    f32 = jnp.float32
    decay_base = jnp.asarray(np.log(2.0 ** (5.0 + np.arange(R_HEADS)) - 1.0), f32)
    return {
        "x_prompt": jax.random.normal(ks[0], (BATCH, SEQ, D_MODEL), f32),
        "x_sample": jax.random.normal(ks[1], (DEC_BATCH, DEC_SEQ, D_MODEL), f32),
        "meta_tokens": jax.random.normal(ks[2], (N_META, D_MODEL), f32),
        "norm_gain": 1.0 + 0.02 * jax.random.normal(ks[3], (DEPTH, D_MODEL), f32),
        "w_in": jax.random.normal(ks[4], (DEPTH, D_MODEL, PROJ_OUT), f32) * D_MODEL ** -0.5,
        "lambda_q1": 0.1 * jax.random.normal(ks[5], (DEPTH, A_QK_DIM), f32),
        "lambda_k1": 0.1 * jax.random.normal(ks[6], (DEPTH, A_QK_DIM), f32),
        "lambda_q2": 0.1 * jax.random.normal(ks[7], (DEPTH, A_QK_DIM), f32),
        "lambda_k2": 0.1 * jax.random.normal(ks[8], (DEPTH, A_QK_DIM), f32),
        "attn_subln_gain": 1.0 + 0.02 * jax.random.normal(ks[9], (DEPTH, A_V_DIM), f32),
        "ret_decay_fwd": decay_base[None] + 0.01 * jax.random.normal(ks[10], (DEPTH, R_HEADS), f32),
        "ret_decay_bwd": decay_base[None] + 0.01 * jax.random.normal(ks[11], (DEPTH, R_HEADS), f32),
        "ret_norm_gain": 1.0 + 0.02 * jax.random.normal(ks[12], (DEPTH, R_HEADS, R_V_DIM), f32),
        "w_out": jax.random.normal(ks[13], (DEPTH, D_MODEL, D_MODEL), f32) * D_MODEL ** -0.5,
        "final_norm_gain": 1.0 + 0.02 * jax.random.normal(ks[14], (D_MODEL,), f32),
    }


def reference(x_prompt, x_sample, meta_tokens, norm_gain, w_in, lambda_q1, lambda_k1, lambda_q2,
              lambda_k2, attn_subln_gain, ret_decay_fwd, ret_decay_bwd, ret_norm_gain, w_out,
              final_norm_gain):
    y_prompt = run_trunk(x_prompt, meta_tokens, norm_gain, w_in, lambda_q1, lambda_k1, lambda_q2,
                         lambda_k2, attn_subln_gain, ret_decay_fwd, ret_decay_bwd, ret_norm_gain,
                         w_out, final_norm_gain)
    y_sample = run_trunk(x_sample, meta_tokens, norm_gain, w_in, lambda_q1, lambda_k1, lambda_q2,
                         lambda_k2, attn_subln_gain, ret_decay_fwd, ret_decay_bwd, ret_norm_gain,
                         w_out, final_norm_gain)
    return (y_prompt, y_sample)
```

```python
import functools
import math

import jax
import jax.numpy as jnp
import numpy as np
from jax import lax
from jax.experimental import pallas as pl
from jax.experimental.pallas import tpu as pltpu

F32 = jnp.float32
BF16 = jnp.bfloat16

D_MODEL = 4096
N_META = 16
CHUNK = 128
A_HEADS = 8
A_QK_DIM = 128
A_V_DIM = 256
A_ROT_DIM = A_QK_DIM // 4
ROPE_THETA = 500000.0
R_HEADS = 8
R_QK_DIM = 128
R_V_DIM = 256
RET_THETA = 10000.0
NORM_EPS = 1e-6
NEG_BIG = -1e30
LAMBDA_INIT = 0.8 - 0.6 * math.exp(-0.3 * 0)

AQ_OFF, AK_OFF, AV_OFF, AG_OFF = 0, 2048, 4096, 6144
RQ_OFF, RK_OFF, RV_OFF, RG_OFF = 8192, 9216, 10240, 12288
PROJ_OUT = 14336

LANES = 128
V7X_VMEM_BYTES = 64 * 1024 * 1024
V7X_VMEM_BUDGET = 56 * 1024 * 1024

PROJ_TM = 1024
PROJ_TN = 1024
ATTN_TQ = 512
ATTN_TK = 512
OUT_TM = 512
OUT_TK = 512
NORM_TM = 512


def _params(sem, vmem_bytes):
    return pltpu.CompilerParams(
        dimension_semantics=sem, vmem_limit_bytes=min(int(vmem_bytes), V7X_VMEM_BUDGET)
    )


def _rmsnorm_kernel(x_ref, g_ref, o_ref):
    x = x_ref[...]
    ms = jnp.mean(x * x, axis=-1, keepdims=True)
    y = x * lax.rsqrt(ms + NORM_EPS)
    o_ref[...] = (y * g_ref[...]).astype(o_ref.dtype)


def _rmsnorm(x2d, gain, tm):
    t, d = x2d.shape
    tm = min(tm, t)
    return pl.pallas_call(
        _rmsnorm_kernel,
        grid=(t // tm,),
        in_specs=[pl.BlockSpec((tm, d), lambda i: (i, 0)), pl.BlockSpec((1, d), lambda i: (0, 0))],
        out_specs=pl.BlockSpec((tm, d), lambda i: (i, 0)),
        out_shape=jax.ShapeDtypeStruct((t, d), BF16),
        compiler_params=_params(("arbitrary",), 2 * tm * d * (4 + 2) + 4 * tm * d * 4),
        name="rmsnorm_in",
    )(x2d, gain.reshape(1, d))


def _proj_kernel(u_ref, w_ref, cosa_ref, sina_ref, sinb_ref, cosr_ref, sinr_ref, kmask_ref, o_ref, *, tn):
    j = pl.program_id(1)
    col0 = j * tn
    acc = jnp.dot(u_ref[...], w_ref[...], preferred_element_type=F32)
    groups = tn // LANES

    is_attn_qk = col0 < AV_OFF
    is_rq = jnp.logical_and(col0 >= RQ_OFF, col0 < RK_OFF)
    is_rk = jnp.logical_and(col0 >= RK_OFF, col0 < RV_OFF)
    is_plain = jnp.logical_not(is_attn_qk | is_rq | is_rk)

    @pl.when(is_plain)
    def _():
        o_ref[...] = acc.astype(o_ref.dtype)

    @pl.when(is_attn_qk)
    def _():
        cos, sin_hi, sin_lo = cosa_ref[...], sina_ref[...], sinb_ref[...]
        for g in range(groups):
            x = acc[:, g * LANES:(g + 1) * LANES]
            y = x * cos + pltpu.roll(x, LANES - A_ROT_DIM // 2, 1) * sin_hi + pltpu.roll(x, A_ROT_DIM // 2, 1) * sin_lo
            o_ref[:, g * LANES:(g + 1) * LANES] = y.astype(o_ref.dtype)

    def ret_rotary(x):
        return x * cosr_ref[...] + pltpu.roll(x, R_QK_DIM // 2, 1) * sinr_ref[...]

    @pl.when(is_rq)
    def _():
        for g in range(groups):
            y = ret_rotary(acc[:, g * LANES:(g + 1) * LANES])
            o_ref[:, g * LANES:(g + 1) * LANES] = y.astype(o_ref.dtype)

    @pl.when(is_rk)
    def _():
        kmask = kmask_ref[...]
        for g in range(groups):
            y = ret_rotary(acc[:, g * LANES:(g + 1) * LANES])
            y = y.astype(F32) * (R_QK_DIM ** -0.5) * kmask
            o_ref[:, g * LANES:(g + 1) * LANES] = y.astype(o_ref.dtype)


def _proj(u, w_bf16, tables, rows_per_seq, tm, tn):
    t, d = u.shape
    tm = min(tm, rows_per_seq)
    n_seq_tiles = rows_per_seq // tm
    tab_spec = pl.BlockSpec((tm, LANES), lambda i, j: (i % n_seq_tiles, 0))
    vmem = 2 * tm * d * 2 + 2 * d * tn * 2 + 2 * tm * tn * 2 + 2 * 6 * tm * LANES * 4 + 3 * tm * tn * 4
    return pl.pallas_call(
        functools.partial(_proj_kernel, tn=tn),
        grid=(t // tm, PROJ_OUT // tn),
        in_specs=[pl.BlockSpec((tm, d), lambda i, j: (i, 0)), pl.BlockSpec((d, tn), lambda i, j: (0, j))]
        + [tab_spec] * 6,
        out_specs=pl.BlockSpec((tm, tn), lambda i, j: (i, j)),
        out_shape=jax.ShapeDtypeStruct((t, PROJ_OUT), BF16),
        compiler_params=_params(("arbitrary", "arbitrary"), vmem),
        name="in_proj",
    )(u, w_bf16, *tables)


def _rotary_tables(pos):
    posf = pos.astype(F32)[:, None]
    lane = jnp.arange(LANES)
    half = A_ROT_DIM // 2
    inv = jnp.power(jnp.float32(ROPE_THETA), -jnp.arange(half, dtype=F32) * 2.0 / A_ROT_DIM)
    ang = posf * inv[None, :]
    cos, sin = jnp.cos(ang), jnp.sin(ang)
    ones = jnp.ones((pos.shape[0], LANES - A_ROT_DIM), F32)
    zeros_h = jnp.zeros((pos.shape[0], half), F32)
    zeros_r = jnp.zeros((pos.shape[0], LANES - A_ROT_DIM), F32)
    cos_a = jnp.concatenate([cos, cos, ones], axis=1)
    sin_hi = jnp.concatenate([-sin, zeros_h, zeros_r], axis=1)
    sin_lo = jnp.concatenate([zeros_h, sin, zeros_r], axis=1)
    half_r = R_QK_DIM // 2
    inv_r = jnp.power(jnp.float32(RET_THETA), -jnp.arange(half_r, dtype=F32) * 2.0 / R_QK_DIM)
    ang_r = posf * inv_r[None, :]
    cos_r = jnp.concatenate([jnp.cos(ang_r), jnp.cos(ang_r)], axis=1)
    sin_r = jnp.concatenate([-jnp.sin(ang_r), jnp.sin(ang_r)], axis=1)
    kmask = jnp.broadcast_to((pos >= 0).astype(F32)[:, None], (pos.shape[0], LANES))
    del lane
    return cos_a, sin_hi, sin_lo, cos_r, sin_r, kmask


def _silu(g):
    return g * (1.0 / (1.0 + jnp.exp(-g)))


def _attn_kernel(q_ref, k_ref, v_ref, g_ref, mk_ref, mv_ref, mbias_ref, lq1_ref, lk1_ref, lq2_ref, lk2_ref,
                 gain_ref, o_ref, m_sc, l_sc, acc_sc, *, tk):
    scale = A_QK_DIM ** -0.5
    n_kb = k_ref.shape[0] // tk
    nt_dims = (((1,), (1,)), ((), ()))
    sub = (pl.ds(0, A_QK_DIM), pl.ds(A_QK_DIM, A_QK_DIM))

    mbias = mbias_ref[...]
    mv = mv_ref[...]
    for idx in range(2):
        q = q_ref[:, sub[idx]]
        s = lax.dot_general(q, mk_ref[:, sub[idx]], nt_dims, preferred_element_type=F32) * scale + mbias
        m = jnp.max(s, axis=-1, keepdims=True)
        p = jnp.exp(s - m)
        m_sc[idx] = m
        l_sc[idx] = jnp.sum(p, axis=-1, keepdims=True)
        acc_sc[idx] = jnp.dot(p.astype(BF16), mv, preferred_element_type=F32)

    def body(kb, carry):
        off = pl.multiple_of(kb * tk, tk)
        v_blk = v_ref[pl.ds(off, tk), :]
        for idx in range(2):
            q = q_ref[:, sub[idx]]
            k_blk = k_ref[pl.ds(off, tk), sub[idx]]
            s = lax.dot_general(q, k_blk, nt_dims, preferred_element_type=F32) * scale
            m_prev = m_sc[idx]
            m_new = jnp.maximum(m_prev, jnp.max(s, axis=-1, keepdims=True))
            alpha = jnp.exp(m_prev - m_new)
            p = jnp.exp(s - m_new)
            l_sc[idx] = alpha * l_sc[idx] + jnp.sum(p, axis=-1, keepdims=True)
            acc_sc[idx] = alpha * acc_sc[idx] + jnp.dot(p.astype(BF16), v_blk, preferred_element_type=F32)
            m_sc[idx] = m_new
        return carry

    lax.fori_loop(0, n_kb, body, 0)

    lam = (jnp.exp(jnp.sum(lq1_ref[...] * lk1_ref[...], axis=-1, keepdims=True))
           - jnp.exp(jnp.sum(lq2_ref[...] * lk2_ref[...], axis=-1, keepdims=True)) + LAMBDA_INIT)
    o = acc_sc[0] / l_sc[0] - lam * (acc_sc[1] / l_sc[1])
    ms = jnp.mean(o * o, axis=-1, keepdims=True)
    y = (o * lax.rsqrt(ms + NORM_EPS)) * gain_ref[...]
    y = y * (1.0 - LAMBDA_INIT)
    o_ref[...] = (y * _silu(g_ref[...].astype(F32))).astype(o_ref.dtype)


def _attention(proj, proj_meta, mbias, lams, subln_gain, batch, seq, tq, tk):
    t = batch * seq
    tq, tk = min(tq, seq), min(tk, seq)
    nq = seq // tq
    hb = A_V_DIM
    q_spec = pl.BlockSpec((tq, hb), lambda b, h, i: (b * nq + i, AQ_OFF // hb + h))
    k_spec = pl.BlockSpec((seq, hb), lambda b, h, i: (b, AK_OFF // hb + h))
    v_spec = pl.BlockSpec((seq, hb), lambda b, h, i: (b, AV_OFF // hb + h))
    g_spec = pl.BlockSpec((tq, hb), lambda b, h, i: (b * nq + i, AG_OFF // hb + h))
    mk_spec = pl.BlockSpec((CHUNK, hb), lambda b, h, i: (0, AK_OFF // hb + h))
    mv_spec = pl.BlockSpec((CHUNK, hb), lambda b, h, i: (0, AV_OFF // hb + h))
    row128 = pl.BlockSpec((1, LANES), lambda b, h, i: (0, 0))
    gain_spec = pl.BlockSpec((1, A_V_DIM), lambda b, h, i: (0, 0))
    vmem = (2 * 2 * seq * hb * 2 + 2 * 3 * tq * hb * 2 + 2 * 2 * CHUNK * hb * 2
            + 2 * tq * (2 * LANES + A_V_DIM) * 4 + 6 * tq * tk * 4)
    return pl.pallas_call(
        functools.partial(_attn_kernel, tk=tk),
        grid=(batch, A_HEADS, nq),
        in_specs=[q_spec, k_spec, v_spec, g_spec, mk_spec, mv_spec, row128, row128, row128, row128, row128,
                  gain_spec],
        out_specs=pl.BlockSpec((tq, hb), lambda b, h, i: (b * nq + i, h)),
        out_shape=jax.ShapeDtypeStruct((t, A_HEADS * A_V_DIM), BF16),
        scratch_shapes=[pltpu.VMEM((2, tq, 1), F32), pltpu.VMEM((2, tq, 1), F32),
                        pltpu.VMEM((2, tq, A_V_DIM), F32)],
        compiler_params=_params(("arbitrary", "arbitrary", "arbitrary"), vmem),
        name="diff_attention",
    )(proj, proj, proj, proj, proj_meta, proj_meta, mbias, *lams, subln_gain.reshape(1, A_V_DIM))


def _log_sigmoid(x):
    return jnp.minimum(x, 0.0) - jnp.log(1.0 + jnp.exp(-jnp.abs(x)))


def _ret_kernel(q_ref, k_ref, v_ref, g_ref, mk_ref, mv_ref, decf_ref, decb_ref, gain_ref, o_ref, fwd_sc):
    c = CHUNK
    n_chunks = q_ref.shape[0] // c
    nt_dims = (((1,), (1,)), ((), ()))
    lg_f = _log_sigmoid(decf_ref[0])
    lg_b = _log_sigmoid(decb_ref[0])
    lg_f_k, lg_b_k = lg_f[:, :R_QK_DIM], lg_b[:, :R_QK_DIM]

    row = lax.broadcasted_iota(jnp.int32, (c, c), 0)
    col = lax.broadcasted_iota(jnp.int32, (c, c), 1)
    rel = (row - col).astype(F32)
    fwd_mask = rel >= 0
    bwd_mask = rel < 0
    d_f = jnp.where(fwd_mask, jnp.exp(jnp.where(fwd_mask, rel, 0.0) * lg_f_k), 0.0)
    d_b = jnp.where(bwd_mask, jnp.exp(jnp.where(bwd_mask, -rel, 0.0) * lg_b_k), 0.0)
    r_v = lax.broadcasted_iota(jnp.int32, (c, R_V_DIM), 0).astype(F32)
    r_k = lax.broadcasted_iota(jnp.int32, (c, R_QK_DIM), 0).astype(F32)
    qd_f = jnp.exp((r_v + 1.0) * lg_f)
    kd_f = jnp.exp((c - 1.0 - r_k) * lg_f_k)
    cd_f = jnp.exp(c * lg_f)
    qd_b = jnp.exp((c - r_v) * lg_b)
    kd_b = jnp.exp(r_k * lg_b_k)
    cd_b = jnp.exp(c * lg_b)

    def state_update(k_blk, v_blk, kd):
        kdec = (k_blk.astype(F32) * kd).T.astype(BF16)
        return jnp.dot(kdec, v_blk, preferred_element_type=F32)

    def chunk_out(q_blk, k_blk, v_blk, state, d_intra, qd):
        s = lax.dot_general(q_blk, k_blk, nt_dims, preferred_element_type=F32) * d_intra
        intra = jnp.dot(s.astype(BF16), v_blk, preferred_element_type=F32)
        inter = jnp.dot(q_blk, state.astype(BF16), preferred_element_type=F32) * qd
        return intra + inter

    state0 = state_update(mk_ref[...], mv_ref[...], kd_f)

    def fwd_body(i, state):
        off = pl.multiple_of(i * c, c)
        q_blk, k_blk, v_blk = q_ref[pl.ds(off, c), :], k_ref[pl.ds(off, c), :], v_ref[pl.ds(off, c), :]
        fwd_sc[pl.ds(off, c), :] = chunk_out(q_blk, k_blk, v_blk, state, d_f, qd_f)
        return state * cd_f + state_update(k_blk, v_blk, kd_f)

    lax.fori_loop(0, n_chunks, fwd_body, state0)

    gain = gain_ref[0]

    def bwd_body(i, state):
        off = pl.multiple_of((n_chunks - 1 - i) * c, c)
        q_blk, k_blk, v_blk = q_ref[pl.ds(off, c), :], k_ref[pl.ds(off, c), :], v_ref[pl.ds(off, c), :]
        ro = fwd_sc[pl.ds(off, c), :] + chunk_out(q_blk, k_blk, v_blk, state, d_b, qd_b)
        ms = jnp.mean(ro * ro, axis=-1, keepdims=True)
        y = (ro * lax.rsqrt(ms + NORM_EPS)) * gain
        o_ref[pl.ds(off, c), :] = (y * _silu(g_ref[pl.ds(off, c), :].astype(F32))).astype(o_ref.dtype)
        return state * cd_b + state_update(k_blk, v_blk, kd_b)

    lax.fori_loop(0, n_chunks, bwd_body, jnp.zeros((R_QK_DIM, R_V_DIM), F32))


def _retention(proj, proj_meta, dec_f, dec_b, ret_gain, batch, seq):
    t = batch * seq
    kb, vb = R_QK_DIM, R_V_DIM
    q_spec = pl.BlockSpec((seq, kb), lambda b, h: (b, RQ_OFF // kb + h))
    k_spec = pl.BlockSpec((seq, kb), lambda b, h: (b, RK_OFF // kb + h))
    v_spec = pl.BlockSpec((seq, vb), lambda b, h: (b, RV_OFF // vb + h))
    g_spec = pl.BlockSpec((seq, vb), lambda b, h: (b, RG_OFF // vb + h))
    mk_spec = pl.BlockSpec((CHUNK, kb), lambda b, h: (0, RK_OFF // kb + h))
    mv_spec = pl.BlockSpec((CHUNK, vb), lambda b, h: (0, RV_OFF // vb + h))
    head_spec = pl.BlockSpec((1, 1, vb), lambda b, h: (h, 0, 0))
    vmem = 2 * seq * (2 * kb + 3 * vb) * 2 + seq * vb * 4 + 32 * CHUNK * vb * 4
    return pl.pallas_call(
        _ret_kernel,
        grid=(batch, R_HEADS),
        in_specs=[q_spec, k_spec, v_spec, g_spec, mk_spec, mv_spec, head_spec, head_spec, head_spec],
        out_specs=pl.BlockSpec((seq, vb), lambda b, h: (b, h)),
        out_shape=jax.ShapeDtypeStruct((t, R_HEADS * R_V_DIM), BF16),
        scratch_shapes=[pltpu.VMEM((seq, vb), F32)],
        compiler_params=_params(("arbitrary", "arbitrary"), vmem),
        name="retention",
    )(proj, proj, proj, proj, proj_meta, proj_meta, dec_f, dec_b, ret_gain)


def _out_kernel(a_ref, r_ref, w_ref, x_ref, g_ref, o_ref, *, n_k):
    k = pl.program_id(1)
    half = n_k // 2

    @pl.when(k == 0)
    def _():
        o_ref[...] = x_ref[...]

    @pl.when(k < half)
    def _():
        o_ref[...] += jnp.dot(a_ref[...], w_ref[...], preferred_element_type=F32)

    @pl.when(k >= half)
    def _():
        o_ref[...] += jnp.dot(r_ref[...], w_ref[...], preferred_element_type=F32)

    @pl.when(k == n_k - 1)
    def _():
        h = o_ref[...]
        ms = jnp.mean(h * h, axis=-1, keepdims=True)
        o_ref[...] = (h * lax.rsqrt(ms + NORM_EPS)) * g_ref[...]


def _out_proj(ao, ro, w_bf16, x2d, gain, tm, tk):
    t, d = x2d.shape
    tm = min(tm, t)
    n_k = d // tk
    half = n_k // 2
    vmem = 2 * 2 * tm * tk * 2 + 2 * tk * d * 2 + 4 * tm * d * 4 + 2 * tm * d * 4
    return pl.pallas_call(
        functools.partial(_out_kernel, n_k=n_k),
        grid=(t // tm, n_k),
        in_specs=[
            pl.BlockSpec((tm, tk), lambda i, k: (i, jnp.minimum(k, half - 1))),
            pl.BlockSpec((tm, tk), lambda i, k: (i, jnp.maximum(k - half, 0))),
            pl.BlockSpec((tk, d), lambda i, k: (k, 0)),
            pl.BlockSpec((tm, d), lambda i, k: (i, 0)),
            pl.BlockSpec((1, d), lambda i, k: (0, 0)),
        ],
        out_specs=pl.BlockSpec((tm, d), lambda i, k: (i, 0)),
        out_shape=jax.ShapeDtypeStruct((t, d), F32),
        compiler_params=_params(("arbitrary", "arbitrary"), vmem),
        name="out_proj",
    )(ao, ro, w_bf16, x2d, gain.reshape(1, d))


def _trunk(x, shared):
    batch, seq, d = x.shape
    x2d = x.reshape(batch * seq, d)
    u = _rmsnorm(x2d, shared["norm_gain"], NORM_TM)
    tables = _rotary_tables(jnp.arange(seq, dtype=jnp.int32) + N_META)
    proj = _proj(u, shared["w_in"], tables, seq, PROJ_TM, PROJ_TN)
    ao = _attention(proj, shared["proj_meta"], shared["mbias"], shared["lams"], shared["subln_gain"],
                    batch, seq, ATTN_TQ, ATTN_TK)
    ro = _retention(proj, shared["proj_meta"], shared["dec_f"], shared["dec_b"], shared["ret_gain"], batch, seq)
    y = _out_proj(ao, ro, shared["w_out"], x2d, shared["final_gain"], OUT_TM, OUT_TK)
    return y.reshape(batch, seq, d)


def kernel(x_prompt, x_sample, meta_tokens, norm_gain, w_in, lambda_q1, lambda_k1, lambda_q2, lambda_k2,
           attn_subln_gain, ret_decay_fwd, ret_decay_bwd, ret_norm_gain, w_out, final_norm_gain):
    assert norm_gain.shape[0] == 1, "single-layer trunk"
    d = x_prompt.shape[-1]
    w_in_bf16 = w_in[0].astype(BF16)
    w_out_bf16 = w_out[0].astype(BF16)

    meta_pos = jnp.arange(CHUNK, dtype=jnp.int32) - (CHUNK - N_META)
    meta_rows = jnp.concatenate([jnp.zeros((CHUNK - N_META, d), F32), meta_tokens.astype(F32)], axis=0)
    u_meta = _rmsnorm(meta_rows, norm_gain[0], CHUNK)
    proj_meta = _proj(u_meta, w_in_bf16, _rotary_tables(meta_pos), CHUNK, CHUNK, PROJ_TN)
    mbias = jnp.where(meta_pos >= 0, 0.0, NEG_BIG).astype(F32).reshape(1, CHUNK)

    def per_head(v):
        return jnp.broadcast_to(v.astype(F32)[:, None, None], (R_HEADS, 1, R_V_DIM))

    shared = dict(
        norm_gain=norm_gain[0], w_in=w_in_bf16, w_out=w_out_bf16, proj_meta=proj_meta, mbias=mbias,
        lams=tuple(v[0].astype(F32).reshape(1, A_QK_DIM) for v in (lambda_q1, lambda_k1, lambda_q2, lambda_k2)),
        subln_gain=attn_subln_gain[0].astype(F32),
        dec_f=per_head(ret_decay_fwd[0]), dec_b=per_head(ret_decay_bwd[0]),
        ret_gain=ret_norm_gain[0].astype(F32).reshape(R_HEADS, 1, R_V_DIM),
        final_gain=final_norm_gain.astype(F32),
    )
    return (_trunk(x_prompt, shared), _trunk(x_sample, shared))
```

```python
import functools
import math

import jax
import jax.numpy as jnp
import numpy as np
from jax import lax
from jax.experimental import pallas as pl
from jax.experimental.pallas import tpu as pltpu

F32 = jnp.float32
BF16 = jnp.bfloat16

D_MODEL = 4096
N_META = 16
CHUNK = 128
A_HEADS = 8
A_QK_DIM = 128
A_V_DIM = 256
A_ROT_DIM = A_QK_DIM // 4
ROPE_THETA = 500000.0
R_HEADS = 8
R_QK_DIM = 128
R_V_DIM = 256
RET_THETA = 10000.0
NORM_EPS = 1e-6
NEG_BIG = -1e30
LAMBDA_INIT = 0.8 - 0.6 * math.exp(-0.3 * 0)

AQ_OFF, AK_OFF, AV_OFF, AG_OFF = 0, 2048, 4096, 6144
RQ_OFF, RK_OFF, RV_OFF, RG_OFF = 8192, 9216, 10240, 12288
PROJ_OUT = 14336

LANES = 128
LOG2E = math.log2(math.e)
V7X_VMEM_BYTES = 64 * 1024 * 1024
V7X_VMEM_BUDGET = 56 * 1024 * 1024

PROJ_TM = 1024
PROJ_TN = 1024
ATTN_TQ = 512
ATTN_TK = 512
ATTN_RT = 128
ATTN_UNROLL = 4
OUT_TM = 512
OUT_TK = 512
NORM_TM = 512


def _params(sem, vmem_bytes):
    return pltpu.CompilerParams(
        dimension_semantics=sem, vmem_limit_bytes=min(int(vmem_bytes), V7X_VMEM_BUDGET)
    )


def _rmsnorm_kernel(x_ref, g_ref, o_ref):
    x = x_ref[...]
    ms = jnp.mean(x * x, axis=-1, keepdims=True)
    y = x * lax.rsqrt(ms + NORM_EPS)
    o_ref[...] = (y * g_ref[...]).astype(o_ref.dtype)


def _rmsnorm(x2d, gain, tm):
    t, d = x2d.shape
    tm = min(tm, t)
    return pl.pallas_call(
        _rmsnorm_kernel,
        grid=(t // tm,),
        in_specs=[pl.BlockSpec((tm, d), lambda i: (i, 0)), pl.BlockSpec((1, d), lambda i: (0, 0))],
        out_specs=pl.BlockSpec((tm, d), lambda i: (i, 0)),
        out_shape=jax.ShapeDtypeStruct((t, d), BF16),
        compiler_params=_params(("arbitrary",), 2 * tm * d * (4 + 2) + 4 * tm * d * 4),
        name="rmsnorm_in",
    )(x2d, gain.reshape(1, d))


def _proj_kernel(u_ref, w_ref, cosa_ref, sina_ref, sinb_ref, cosr_ref, sinr_ref, kmask_ref, o_ref, *, tn):
    j = pl.program_id(1)
    col0 = j * tn
    acc = jnp.dot(u_ref[...], w_ref[...], preferred_element_type=F32)
    groups = tn // LANES

    is_attn_qk = col0 < AV_OFF
    is_rq = jnp.logical_and(col0 >= RQ_OFF, col0 < RK_OFF)
    is_rk = jnp.logical_and(col0 >= RK_OFF, col0 < RV_OFF)
    is_plain = jnp.logical_not(is_attn_qk | is_rq | is_rk)

    @pl.when(is_plain)
    def _():
        o_ref[...] = acc.astype(o_ref.dtype)

    @pl.when(is_attn_qk)
    def _():
        cos, sin_hi, sin_lo = cosa_ref[...], sina_ref[...], sinb_ref[...]
        for g in range(groups):
            x = acc[:, g * LANES:(g + 1) * LANES]
            y = x * cos + pltpu.roll(x, LANES - A_ROT_DIM // 2, 1) * sin_hi + pltpu.roll(x, A_ROT_DIM // 2, 1) * sin_lo
            o_ref[:, g * LANES:(g + 1) * LANES] = y.astype(o_ref.dtype)

    def ret_rotary(x):
        return x * cosr_ref[...] + pltpu.roll(x, R_QK_DIM // 2, 1) * sinr_ref[...]

    @pl.when(is_rq)
    def _():
        for g in range(groups):
            y = ret_rotary(acc[:, g * LANES:(g + 1) * LANES])
            o_ref[:, g * LANES:(g + 1) * LANES] = y.astype(o_ref.dtype)

    @pl.when(is_rk)
    def _():
        kmask = kmask_ref[...]
        for g in range(groups):
            y = ret_rotary(acc[:, g * LANES:(g + 1) * LANES])
            y = y * (R_QK_DIM ** -0.5) * kmask
            o_ref[:, g * LANES:(g + 1) * LANES] = y.astype(o_ref.dtype)


def _proj(u, w_bf16, tables, rows_per_seq, tm, tn):
    t, d = u.shape
    tm = min(tm, rows_per_seq)
    n_seq_tiles = rows_per_seq // tm
    tab_spec = pl.BlockSpec((tm, LANES), lambda i, j: (i % n_seq_tiles, 0))
    vmem = 2 * tm * d * 2 + 2 * d * tn * 2 + 2 * tm * tn * 2 + 2 * 6 * tm * LANES * 4 + 3 * tm * tn * 4
    return pl.pallas_call(
        functools.partial(_proj_kernel, tn=tn),
        grid=(t // tm, PROJ_OUT // tn),
        in_specs=[pl.BlockSpec((tm, d), lambda i, j: (i, 0)), pl.BlockSpec((d, tn), lambda i, j: (0, j))]
        + [tab_spec] * 6,
        out_specs=pl.BlockSpec((tm, tn), lambda i, j: (i, j)),
        out_shape=jax.ShapeDtypeStruct((t, PROJ_OUT), BF16),
        compiler_params=_params(("arbitrary", "arbitrary"), vmem),
        name="in_proj",
    )(u, w_bf16, *tables)


def _rotary_tables(pos):
    posf = pos.astype(F32)[:, None]
    lane = jnp.arange(LANES)
    half = A_ROT_DIM // 2
    inv = jnp.power(jnp.float32(ROPE_THETA), -jnp.arange(half, dtype=F32) * 2.0 / A_ROT_DIM)
    ang = posf * inv[None, :]
    cos, sin = jnp.cos(ang), jnp.sin(ang)
    ones = jnp.ones((pos.shape[0], LANES - A_ROT_DIM), F32)
    zeros_h = jnp.zeros((pos.shape[0], half), F32)
    zeros_r = jnp.zeros((pos.shape[0], LANES - A_ROT_DIM), F32)
    cos_a = jnp.concatenate([cos, cos, ones], axis=1)
    sin_hi = jnp.concatenate([-sin, zeros_h, zeros_r], axis=1)
    sin_lo = jnp.concatenate([zeros_h, sin, zeros_r], axis=1)
    half_r = R_QK_DIM // 2
    inv_r = jnp.power(jnp.float32(RET_THETA), -jnp.arange(half_r, dtype=F32) * 2.0 / R_QK_DIM)
    ang_r = posf * inv_r[None, :]
    cos_r = jnp.concatenate([jnp.cos(ang_r), jnp.cos(ang_r)], axis=1)
    sin_r = jnp.concatenate([-jnp.sin(ang_r), jnp.sin(ang_r)], axis=1)
    kmask = jnp.broadcast_to((pos >= 0).astype(F32)[:, None], (pos.shape[0], LANES))
    del lane
    return cos_a, sin_hi, sin_lo, cos_r, sin_r, kmask


def _silu(g):
    return g * (1.0 / (1.0 + jnp.exp(-g)))


def _attn_kernel(q_ref, k_ref, v_ref, g_ref, mk_ref, mv_ref, mbias_ref, lq1_ref, lk1_ref, lq2_ref, lk2_ref,
                 gain_ref, o_ref, m_sc, l_sc, acc_sc, *, tk, rt):
    c = (A_QK_DIM ** -0.5) * LOG2E
    tq = q_ref.shape[0]
    n_kb = k_ref.shape[0] // tk
    nt_dims = (((1,), (1,)), ((), ()))
    sub = (pl.ds(0, A_QK_DIM), pl.ds(A_QK_DIM, A_QK_DIM))

    def lane_tiles(x):
        return [x[:, t * LANES:(t + 1) * LANES] for t in range(x.shape[1] // LANES)]

    def update(idx, rows, s, v_blk, first):
        tiles = lane_tiles(s)
        mx = functools.reduce(jnp.maximum, tiles)
        m_cur = jnp.max(mx, axis=-1, keepdims=True)
        if first:
            m_new = jnp.broadcast_to(m_cur, (rt, LANES))
        else:
            m_prev = m_sc[idx, rows, :]
            m_new = jnp.maximum(m_prev, m_cur)
        p_tiles = [jnp.exp2(t - m_new) for t in tiles]
        p = jnp.concatenate(p_tiles, axis=1) if len(p_tiles) > 1 else p_tiles[0]
        pv = jnp.dot(p.astype(BF16), v_blk, preferred_element_type=F32)
        l_part = functools.reduce(jnp.add, p_tiles)
        if first:
            l_sc[idx, rows, :] = l_part
            acc_sc[idx, rows, :] = pv
        else:
            alpha = jnp.exp2(m_prev - m_new)
            l_sc[idx, rows, :] = alpha * l_sc[idx, rows, :] + l_part
            acc_sc[idx, rows, :] = jnp.concatenate([alpha, alpha], axis=1) * acc_sc[idx, rows, :] + pv
        m_sc[idx, rows, :] = m_new

    def scores(rows, idx, keys):
        return lax.dot_general(q_ref[rows, sub[idx]], keys, nt_dims, preferred_element_type=F32) * c

    mbias = mbias_ref[...] * LOG2E
    v_first = jnp.concatenate([mv_ref[...], v_ref[pl.ds(0, tk), :]], axis=0)
    for r in range(tq // rt):
        rows = pl.ds(r * rt, rt)
        for idx in range(2):
            s_meta = scores(rows, idx, mk_ref[:, sub[idx]]) + mbias
            s_tok = scores(rows, idx, k_ref[pl.ds(0, tk), sub[idx]])
            update(idx, rows, jnp.concatenate([s_meta, s_tok], axis=1), v_first, True)

    def body(kb, carry):
        off = pl.multiple_of(kb * tk, tk)
        v_blk = v_ref[pl.ds(off, tk), :]
        for r in range(tq // rt):
            rows = pl.ds(r * rt, rt)
            for idx in range(2):
                update(idx, rows, scores(rows, idx, k_ref[pl.ds(off, tk), sub[idx]]), v_blk, False)
        return carry

    lax.fori_loop(1, n_kb, body, 0, unroll=ATTN_UNROLL)

    lam = (jnp.exp(jnp.sum(lq1_ref[...] * lk1_ref[...], axis=-1, keepdims=True))
           - jnp.exp(jnp.sum(lq2_ref[...] * lk2_ref[...], axis=-1, keepdims=True)) + LAMBDA_INIT)
    l1 = jnp.sum(l_sc[0], axis=-1, keepdims=True)
    l2 = jnp.sum(l_sc[1], axis=-1, keepdims=True)
    o = acc_sc[0] / l1 - lam * (acc_sc[1] / l2)
    ms = jnp.mean(o * o, axis=-1, keepdims=True)
    y = (o * lax.rsqrt(ms + NORM_EPS)) * gain_ref[...]
    y = y * (1.0 - LAMBDA_INIT)
    o_ref[...] = (y * _silu(g_ref[...].astype(F32))).astype(o_ref.dtype)


def _attention(proj, proj_meta, mbias, lams, subln_gain, batch, seq, tq, tk):
    t = batch * seq
    tq, tk = min(tq, seq), min(tk, seq)
    nq = seq // tq
    hb = A_V_DIM
    q_spec = pl.BlockSpec((tq, hb), lambda b, h, i: (b * nq + i, AQ_OFF // hb + h))
    k_spec = pl.BlockSpec((seq, hb), lambda b, h, i: (b, AK_OFF // hb + h))
    v_spec = pl.BlockSpec((seq, hb), lambda b, h, i: (b, AV_OFF // hb + h))
    g_spec = pl.BlockSpec((tq, hb), lambda b, h, i: (b * nq + i, AG_OFF // hb + h))
    mk_spec = pl.BlockSpec((CHUNK, hb), lambda b, h, i: (0, AK_OFF // hb + h))
    mv_spec = pl.BlockSpec((CHUNK, hb), lambda b, h, i: (0, AV_OFF // hb + h))
    row128 = pl.BlockSpec((1, LANES), lambda b, h, i: (0, 0))
    gain_spec = pl.BlockSpec((1, A_V_DIM), lambda b, h, i: (0, 0))
    vmem = (2 * 2 * seq * hb * 2 + 2 * 3 * tq * hb * 2 + 2 * 2 * CHUNK * hb * 2
            + 2 * tq * (2 * LANES + A_V_DIM) * 4 + 6 * tq * tk * 4)
    return pl.pallas_call(
        functools.partial(_attn_kernel, tk=tk, rt=min(ATTN_RT, tq)),
        grid=(batch, A_HEADS, nq),
        in_specs=[q_spec, k_spec, v_spec, g_spec, mk_spec, mv_spec, row128, row128, row128, row128, row128,
                  gain_spec],
        out_specs=pl.BlockSpec((tq, hb), lambda b, h, i: (b * nq + i, h)),
        out_shape=jax.ShapeDtypeStruct((t, A_HEADS * A_V_DIM), BF16),
        scratch_shapes=[pltpu.VMEM((2, tq, LANES), F32), pltpu.VMEM((2, tq, LANES), F32),
                        pltpu.VMEM((2, tq, A_V_DIM), F32)],
        compiler_params=_params(("arbitrary", "arbitrary", "arbitrary"), vmem),
        name="diff_attention",
    )(proj, proj, proj, proj, proj_meta, proj_meta, mbias, *lams, subln_gain.reshape(1, A_V_DIM))


def _log_sigmoid(x):
    return jnp.minimum(x, 0.0) - jnp.log(1.0 + jnp.exp(-jnp.abs(x)))


def _ret_kernel(q_ref, k_ref, v_ref, g_ref, mk_ref, mv_ref, decf_ref, decb_ref, gain_ref, o_ref, fwd_sc):
    c = CHUNK
    n_chunks = q_ref.shape[0] // c
    nt_dims = (((1,), (1,)), ((), ()))
    lg_f = _log_sigmoid(decf_ref[0])
    lg_b = _log_sigmoid(decb_ref[0])
    lg_f_k, lg_b_k = lg_f[:, :R_QK_DIM], lg_b[:, :R_QK_DIM]

    row = lax.broadcasted_iota(jnp.int32, (c, c), 0)
    col = lax.broadcasted_iota(jnp.int32, (c, c), 1)
    rel = (row - col).astype(F32)
    fwd_mask = rel >= 0
    bwd_mask = rel < 0
    d_f = jnp.where(fwd_mask, jnp.exp(jnp.where(fwd_mask, rel, 0.0) * lg_f_k), 0.0)
    d_b = jnp.where(bwd_mask, jnp.exp(jnp.where(bwd_mask, -rel, 0.0) * lg_b_k), 0.0)
    r_v = lax.broadcasted_iota(jnp.int32, (c, R_V_DIM), 0).astype(F32)
    r_k = lax.broadcasted_iota(jnp.int32, (c, R_QK_DIM), 0).astype(F32)
    qd_f = jnp.exp((r_v + 1.0) * lg_f)
    kd_f = jnp.exp((c - 1.0 - r_k) * lg_f_k)
    cd_f = jnp.exp(c * lg_f)
    qd_b = jnp.exp((c - r_v) * lg_b)
    kd_b = jnp.exp(r_k * lg_b_k)
    cd_b = jnp.exp(c * lg_b)

    def state_update(k_blk, v_blk, kd):
        kdec = (k_blk.astype(F32) * kd).T.astype(BF16)
        return jnp.dot(kdec, v_blk, preferred_element_type=F32)

    def chunk_out(q_blk, k_blk, v_blk, state, d_intra, qd):
        s = lax.dot_general(q_blk, k_blk, nt_dims, preferred_element_type=F32) * d_intra
        intra = jnp.dot(s.astype(BF16), v_blk, preferred_element_type=F32)
        inter = jnp.dot(q_blk, state.astype(BF16), preferred_element_type=F32) * qd
        return intra + inter

    state0 = state_update(mk_ref[...], mv_ref[...], kd_f)

    def fwd_body(i, state):
        off = pl.multiple_of(i * c, c)
        q_blk, k_blk, v_blk = q_ref[pl.ds(off, c), :], k_ref[pl.ds(off, c), :], v_ref[pl.ds(off, c), :]
        fwd_sc[pl.ds(off, c), :] = chunk_out(q_blk, k_blk, v_blk, state, d_f, qd_f)
        return state * cd_f + state_update(k_blk, v_blk, kd_f)

    lax.fori_loop(0, n_chunks, fwd_body, state0)

    gain = gain_ref[0]

    def bwd_body(i, state):
        off = pl.multiple_of((n_chunks - 1 - i) * c, c)
        q_blk, k_blk, v_blk = q_ref[pl.ds(off, c), :], k_ref[pl.ds(off, c), :], v_ref[pl.ds(off, c), :]
        ro = fwd_sc[pl.ds(off, c), :] + chunk_out(q_blk, k_blk, v_blk, state, d_b, qd_b)
        ms = jnp.mean(ro * ro, axis=-1, keepdims=True)
        y = (ro * lax.rsqrt(ms + NORM_EPS)) * gain
        o_ref[pl.ds(off, c), :] = (y * _silu(g_ref[pl.ds(off, c), :].astype(F32))).astype(o_ref.dtype)
        return state * cd_b + state_update(k_blk, v_blk, kd_b)

    lax.fori_loop(0, n_chunks, bwd_body, jnp.zeros((R_QK_DIM, R_V_DIM), F32))


def _retention(proj, proj_meta, dec_f, dec_b, ret_gain, batch, seq):
    t = batch * seq
    kb, vb = R_QK_DIM, R_V_DIM
    q_spec = pl.BlockSpec((seq, kb), lambda b, h: (b, RQ_OFF // kb + h))
    k_spec = pl.BlockSpec((seq, kb), lambda b, h: (b, RK_OFF // kb + h))
    v_spec = pl.BlockSpec((seq, vb), lambda b, h: (b, RV_OFF // vb + h))
    g_spec = pl.BlockSpec((seq, vb), lambda b, h: (b, RG_OFF // vb + h))
    mk_spec = pl.BlockSpec((CHUNK, kb), lambda b, h: (0, RK_OFF // kb + h))
    mv_spec = pl.BlockSpec((CHUNK, vb), lambda b, h: (0, RV_OFF // vb + h))
    head_spec = pl.BlockSpec((1, 1, vb), lambda b, h: (h, 0, 0))
    vmem = 2 * seq * (2 * kb + 3 * vb) * 2 + seq * vb * 4 + 32 * CHUNK * vb * 4
    return pl.pallas_call(
        _ret_kernel,
        grid=(batch, R_HEADS),
        in_specs=[q_spec, k_spec, v_spec, g_spec, mk_spec, mv_spec, head_spec, head_spec, head_spec],
        out_specs=pl.BlockSpec((seq, vb), lambda b, h: (b, h)),
        out_shape=jax.ShapeDtypeStruct((t, R_HEADS * R_V_DIM), BF16),
        scratch_shapes=[pltpu.VMEM((seq, vb), F32)],
        compiler_params=_params(("arbitrary", "arbitrary"), vmem),
        name="retention",
    )(proj, proj, proj, proj, proj_meta, proj_meta, dec_f, dec_b, ret_gain)


def _out_kernel(a_ref, r_ref, w_ref, x_ref, g_ref, o_ref, *, n_k):
    k = pl.program_id(1)
    half = n_k // 2

    @pl.when(k == 0)
    def _():
        o_ref[...] = x_ref[...]

    @pl.when(k < half)
    def _():
        o_ref[...] += jnp.dot(a_ref[...], w_ref[...], preferred_element_type=F32)

    @pl.when(k >= half)
    def _():
        o_ref[...] += jnp.dot(r_ref[...], w_ref[...], preferred_element_type=F32)

    @pl.when(k == n_k - 1)
    def _():
        h = o_ref[...]
        ms = jnp.mean(h * h, axis=-1, keepdims=True)
        o_ref[...] = (h * lax.rsqrt(ms + NORM_EPS)) * g_ref[...]


def _out_proj(ao, ro, w_bf16, x2d, gain, tm, tk):
    t, d = x2d.shape
    tm = min(tm, t)
    n_k = d // tk
    half = n_k // 2
    vmem = 2 * 2 * tm * tk * 2 + 2 * tk * d * 2 + 4 * tm * d * 4 + 2 * tm * d * 4
    return pl.pallas_call(
        functools.partial(_out_kernel, n_k=n_k),
        grid=(t // tm, n_k),
        in_specs=[
            pl.BlockSpec((tm, tk), lambda i, k: (i, jnp.minimum(k, half - 1))),
            pl.BlockSpec((tm, tk), lambda i, k: (i, jnp.maximum(k - half, 0))),
            pl.BlockSpec((tk, d), lambda i, k: (k, 0)),
            pl.BlockSpec((tm, d), lambda i, k: (i, 0)),
            pl.BlockSpec((1, d), lambda i, k: (0, 0)),
        ],
        out_specs=pl.BlockSpec((tm, d), lambda i, k: (i, 0)),
        out_shape=jax.ShapeDtypeStruct((t, d), F32),
        compiler_params=_params(("arbitrary", "arbitrary"), vmem),
        name="out_proj",
    )(ao, ro, w_bf16, x2d, gain.reshape(1, d))


def _trunk(x, shared):
    batch, seq, d = x.shape
    x2d = x.reshape(batch * seq, d)
    u = _rmsnorm(x2d, shared["norm_gain"], NORM_TM)
    tables = _rotary_tables(jnp.arange(seq, dtype=jnp.int32) + N_META)
    proj = _proj(u, shared["w_in"], tables, seq, PROJ_TM, PROJ_TN)
    ao = _attention(proj, shared["proj_meta"], shared["mbias"], shared["lams"], shared["subln_gain"],
                    batch, seq, ATTN_TQ, ATTN_TK)
    ro = _retention(proj, shared["proj_meta"], shared["dec_f"], shared["dec_b"], shared["ret_gain"], batch, seq)
    y = _out_proj(ao, ro, shared["w_out"], x2d, shared["final_gain"], OUT_TM, OUT_TK)
    return y.reshape(batch, seq, d)


def kernel(x_prompt, x_sample, meta_tokens, norm_gain, w_in, lambda_q1, lambda_k1, lambda_q2, lambda_k2,
           attn_subln_gain, ret_decay_fwd, ret_decay_bwd, ret_norm_gain, w_out, final_norm_gain):
    assert norm_gain.shape[0] == 1, "single-layer trunk"
    d = x_prompt.shape[-1]
    w_in_bf16 = w_in[0].astype(BF16)
    w_out_bf16 = w_out[0].astype(BF16)

    meta_pos = jnp.arange(CHUNK, dtype=jnp.int32) - (CHUNK - N_META)
    meta_rows = jnp.concatenate([jnp.zeros((CHUNK - N_META, d), F32), meta_tokens.astype(F32)], axis=0)
    u_meta = _rmsnorm(meta_rows, norm_gain[0], CHUNK)
    proj_meta = _proj(u_meta, w_in_bf16, _rotary_tables(meta_pos), CHUNK, CHUNK, PROJ_TN)
    mbias = jnp.where(meta_pos >= 0, 0.0, NEG_BIG).astype(F32).reshape(1, CHUNK)

    def per_head(v):
        return jnp.broadcast_to(v.astype(F32)[:, None, None], (R_HEADS, 1, R_V_DIM))

    shared = dict(
        norm_gain=norm_gain[0], w_in=w_in_bf16, w_out=w_out_bf16, proj_meta=proj_meta, mbias=mbias,
        lams=tuple(v[0].astype(F32).reshape(1, A_QK_DIM) for v in (lambda_q1, lambda_k1, lambda_q2, lambda_k2)),
        subln_gain=attn_subln_gain[0].astype(F32),
        dec_f=per_head(ret_decay_fwd[0]), dec_b=per_head(ret_decay_bwd[0]),
        ret_gain=ret_norm_gain[0].astype(F32).reshape(R_HEADS, 1, R_V_DIM),
        final_gain=final_norm_gain.astype(F32),
    )
    return (_trunk(x_prompt, shared), _trunk(x_sample, shared))
```

```python
import functools
import math

import jax
import jax.numpy as jnp
import numpy as np
from jax import lax
from jax.experimental import pallas as pl
from jax.experimental.pallas import tpu as pltpu

F32 = jnp.float32
BF16 = jnp.bfloat16

D_MODEL = 4096
N_META = 16
CHUNK = 128
A_HEADS = 8
A_QK_DIM = 128
A_V_DIM = 256
A_ROT_DIM = A_QK_DIM // 4
ROPE_THETA = 500000.0
R_HEADS = 8
R_QK_DIM = 128
R_V_DIM = 256
RET_THETA = 10000.0
NORM_EPS = 1e-6
NEG_BIG = -1e30
LAMBDA_INIT = 0.8 - 0.6 * math.exp(-0.3 * 0)

AQ_OFF, AK_OFF, AV_OFF, AG_OFF = 0, 2048, 4096, 6144
RQ_OFF, RK_OFF, RV_OFF, RG_OFF = 8192, 9216, 10240, 12288
PROJ_OUT = 14336

LANES = 128
LOG2E = math.log2(math.e)
V7X_VMEM_BYTES = 64 * 1024 * 1024
V7X_VMEM_BUDGET = 60 * 1024 * 1024

PROJ_TM = 1024
PROJ_TN = 1024
ATTN_TQ = 512
ATTN_TK = 512
ATTN_RT = 128
ATTN_UNROLL = 4
OUT_TM = 256
RET_CHUNK = 256
RET_UNROLL = 4
NORM_TM = 512


def _params(sem, vmem_bytes):
    return pltpu.CompilerParams(
        dimension_semantics=sem, vmem_limit_bytes=min(int(vmem_bytes), V7X_VMEM_BUDGET)
    )


def _rmsnorm_kernel(x_ref, g_ref, o_ref):
    x = x_ref[...]
    ms = jnp.mean(x * x, axis=-1, keepdims=True)
    y = x * lax.rsqrt(ms + NORM_EPS)
    o_ref[...] = (y * g_ref[...]).astype(o_ref.dtype)


def _rmsnorm(x2d, gain, tm):
    t, d = x2d.shape
    tm = min(tm, t)
    return pl.pallas_call(
        _rmsnorm_kernel,
        grid=(t // tm,),
        in_specs=[pl.BlockSpec((tm, d), lambda i: (i, 0)), pl.BlockSpec((1, d), lambda i: (0, 0))],
        out_specs=pl.BlockSpec((tm, d), lambda i: (i, 0)),
        out_shape=jax.ShapeDtypeStruct((t, d), BF16),
        compiler_params=_params(("arbitrary",), 2 * tm * d * (4 + 2) + 4 * tm * d * 4),
        name="rmsnorm_in",
    )(x2d, gain.reshape(1, d))


def _proj_kernel(u_ref, w_ref, cosa_ref, sina_ref, sinb_ref, cosr_ref, sinr_ref, kmask_ref, o_ref, *, tn):
    j = pl.program_id(1)
    col0 = j * tn
    acc = jnp.dot(u_ref[...], w_ref[...], preferred_element_type=F32)
    groups = tn // LANES

    is_attn_qk = col0 < AV_OFF
    is_rq = jnp.logical_and(col0 >= RQ_OFF, col0 < RK_OFF)
    is_rk = jnp.logical_and(col0 >= RK_OFF, col0 < RV_OFF)
    is_plain = jnp.logical_not(is_attn_qk | is_rq | is_rk)

    @pl.when(is_plain)
    def _():
        o_ref[...] = acc.astype(o_ref.dtype)

    @pl.when(is_attn_qk)
    def _():
        cos, sin_hi, sin_lo = cosa_ref[...], sina_ref[...], sinb_ref[...]
        for g in range(groups):
            x = acc[:, g * LANES:(g + 1) * LANES]
            y = x * cos + pltpu.roll(x, LANES - A_ROT_DIM // 2, 1) * sin_hi + pltpu.roll(x, A_ROT_DIM // 2, 1) * sin_lo
            o_ref[:, g * LANES:(g + 1) * LANES] = y.astype(o_ref.dtype)

    def ret_rotary(x):
        return x * cosr_ref[...] + pltpu.roll(x, R_QK_DIM // 2, 1) * sinr_ref[...]

    @pl.when(is_rq)
    def _():
        for g in range(groups):
            y = ret_rotary(acc[:, g * LANES:(g + 1) * LANES])
            o_ref[:, g * LANES:(g + 1) * LANES] = y.astype(o_ref.dtype)

    @pl.when(is_rk)
    def _():
        kmask = kmask_ref[...]
        for g in range(groups):
            y = ret_rotary(acc[:, g * LANES:(g + 1) * LANES])
            y = y * (R_QK_DIM ** -0.5) * kmask
            o_ref[:, g * LANES:(g + 1) * LANES] = y.astype(o_ref.dtype)


def _proj(u, w_bf16, tables, rows_per_seq, tm, tn):
    t, d = u.shape
    tm = min(tm, rows_per_seq)
    n_seq_tiles = rows_per_seq // tm
    tab_spec = pl.BlockSpec((tm, LANES), lambda i, j: (i % n_seq_tiles, 0))
    vmem = 2 * tm * d * 2 + 2 * d * tn * 2 + 2 * tm * tn * 2 + 2 * 6 * tm * LANES * 4 + 3 * tm * tn * 4
    return pl.pallas_call(
        functools.partial(_proj_kernel, tn=tn),
        grid=(t // tm, PROJ_OUT // tn),
        in_specs=[pl.BlockSpec((tm, d), lambda i, j: (i, 0)), pl.BlockSpec((d, tn), lambda i, j: (0, j))]
        + [tab_spec] * 6,
        out_specs=pl.BlockSpec((tm, tn), lambda i, j: (i, j)),
        out_shape=jax.ShapeDtypeStruct((t, PROJ_OUT), BF16),
        compiler_params=_params(("arbitrary", "arbitrary"), vmem),
        name="in_proj",
    )(u, w_bf16, *tables)


def _rotary_tables(pos):
    posf = pos.astype(F32)[:, None]
    lane = jnp.arange(LANES)
    half = A_ROT_DIM // 2
    inv = jnp.power(jnp.float32(ROPE_THETA), -jnp.arange(half, dtype=F32) * 2.0 / A_ROT_DIM)
    ang = posf * inv[None, :]
    cos, sin = jnp.cos(ang), jnp.sin(ang)
    ones = jnp.ones((pos.shape[0], LANES - A_ROT_DIM), F32)
    zeros_h = jnp.zeros((pos.shape[0], half), F32)
    zeros_r = jnp.zeros((pos.shape[0], LANES - A_ROT_DIM), F32)
    cos_a = jnp.concatenate([cos, cos, ones], axis=1)
    sin_hi = jnp.concatenate([-sin, zeros_h, zeros_r], axis=1)
    sin_lo = jnp.concatenate([zeros_h, sin, zeros_r], axis=1)
    half_r = R_QK_DIM // 2
    inv_r = jnp.power(jnp.float32(RET_THETA), -jnp.arange(half_r, dtype=F32) * 2.0 / R_QK_DIM)
    ang_r = posf * inv_r[None, :]
    cos_r = jnp.concatenate([jnp.cos(ang_r), jnp.cos(ang_r)], axis=1)
    sin_r = jnp.concatenate([-jnp.sin(ang_r), jnp.sin(ang_r)], axis=1)
    kmask = jnp.broadcast_to((pos >= 0).astype(F32)[:, None], (pos.shape[0], LANES))
    del lane
    return cos_a, sin_hi, sin_lo, cos_r, sin_r, kmask


def _silu(g):
    return g * (1.0 / (1.0 + jnp.exp(-g)))


def _attn_kernel(q_ref, k_ref, v_ref, g_ref, mk_ref, mv_ref, mbias_ref, lq1_ref, lk1_ref, lq2_ref, lk2_ref,
                 gain_ref, o_ref, m_sc, l_sc, acc_sc, *, tk, rt):
    c = (A_QK_DIM ** -0.5) * LOG2E
    tq = q_ref.shape[0]
    n_kb = k_ref.shape[0] // tk
    nt_dims = (((1,), (1,)), ((), ()))
    sub = (pl.ds(0, A_QK_DIM), pl.ds(A_QK_DIM, A_QK_DIM))

    def lane_tiles(x):
        return [x[:, t * LANES:(t + 1) * LANES] for t in range(x.shape[1] // LANES)]

    def update(idx, rows, s, v_blk, first):
        tiles = lane_tiles(s)
        mx = functools.reduce(jnp.maximum, tiles)
        m_cur = jnp.max(mx, axis=-1, keepdims=True)
        if first:
            m_new = jnp.broadcast_to(m_cur, (rt, LANES))
        else:
            m_prev = m_sc[idx, rows, :]
            m_new = jnp.maximum(m_prev, m_cur)
        p_tiles = [jnp.exp2(t - m_new) for t in tiles]
        p = jnp.concatenate(p_tiles, axis=1) if len(p_tiles) > 1 else p_tiles[0]
        pv = jnp.dot(p.astype(BF16), v_blk, preferred_element_type=F32)
        l_part = functools.reduce(jnp.add, p_tiles)
        if first:
            l_sc[idx, rows, :] = l_part
            acc_sc[idx, rows, :] = pv
        else:
            alpha = jnp.exp2(m_prev - m_new)
            l_sc[idx, rows, :] = alpha * l_sc[idx, rows, :] + l_part
            acc_sc[idx, rows, :] = jnp.concatenate([alpha, alpha], axis=1) * acc_sc[idx, rows, :] + pv
        m_sc[idx, rows, :] = m_new

    def scores(rows, idx, keys):
        return lax.dot_general(q_ref[rows, sub[idx]], keys, nt_dims, preferred_element_type=F32) * c

    mbias = mbias_ref[...] * LOG2E
    v_first = jnp.concatenate([mv_ref[...], v_ref[pl.ds(0, tk), :]], axis=0)
    for r in range(tq // rt):
        rows = pl.ds(r * rt, rt)
        for idx in range(2):
            s_meta = scores(rows, idx, mk_ref[:, sub[idx]]) + mbias
            s_tok = scores(rows, idx, k_ref[pl.ds(0, tk), sub[idx]])
            update(idx, rows, jnp.concatenate([s_meta, s_tok], axis=1), v_first, True)

    def body(kb, carry):
        off = pl.multiple_of(kb * tk, tk)
        v_blk = v_ref[pl.ds(off, tk), :]
        for r in range(tq // rt):
            rows = pl.ds(r * rt, rt)
            for idx in range(2):
                update(idx, rows, scores(rows, idx, k_ref[pl.ds(off, tk), sub[idx]]), v_blk, False)
        return carry

    lax.fori_loop(1, n_kb, body, 0, unroll=ATTN_UNROLL)

    lam = (jnp.exp(jnp.sum(lq1_ref[...] * lk1_ref[...], axis=-1, keepdims=True))
           - jnp.exp(jnp.sum(lq2_ref[...] * lk2_ref[...], axis=-1, keepdims=True)) + LAMBDA_INIT)
    l1 = jnp.sum(l_sc[0], axis=-1, keepdims=True)
    l2 = jnp.sum(l_sc[1], axis=-1, keepdims=True)
    o = acc_sc[0] / l1 - lam * (acc_sc[1] / l2)
    ms = jnp.mean(o * o, axis=-1, keepdims=True)
    y = (o * lax.rsqrt(ms + NORM_EPS)) * gain_ref[...]
    y = y * (1.0 - LAMBDA_INIT)
    o_ref[...] = (y * _silu(g_ref[...].astype(F32))).astype(o_ref.dtype)


def _attention(proj, proj_meta, mbias, lams, subln_gain, batch, seq, tq, tk):
    t = batch * seq
    tq, tk = min(tq, seq), min(tk, seq)
    nq = seq // tq
    hb = A_V_DIM
    q_spec = pl.BlockSpec((tq, hb), lambda b, h, i: (b * nq + i, AQ_OFF // hb + h))
    k_spec = pl.BlockSpec((seq, hb), lambda b, h, i: (b, AK_OFF // hb + h))
    v_spec = pl.BlockSpec((seq, hb), lambda b, h, i: (b, AV_OFF // hb + h))
    g_spec = pl.BlockSpec((tq, hb), lambda b, h, i: (b * nq + i, AG_OFF // hb + h))
    mk_spec = pl.BlockSpec((CHUNK, hb), lambda b, h, i: (0, AK_OFF // hb + h))
    mv_spec = pl.BlockSpec((CHUNK, hb), lambda b, h, i: (0, AV_OFF // hb + h))
    row128 = pl.BlockSpec((1, LANES), lambda b, h, i: (0, 0))
    gain_spec = pl.BlockSpec((1, A_V_DIM), lambda b, h, i: (0, 0))
    vmem = (2 * 2 * seq * hb * 2 + 2 * 3 * tq * hb * 2 + 2 * 2 * CHUNK * hb * 2
            + 2 * tq * (2 * LANES + A_V_DIM) * 4 + 6 * tq * tk * 4)
    return pl.pallas_call(
        functools.partial(_attn_kernel, tk=tk, rt=min(ATTN_RT, tq)),
        grid=(batch, A_HEADS, nq),
        in_specs=[q_spec, k_spec, v_spec, g_spec, mk_spec, mv_spec, row128, row128, row128, row128, row128,
                  gain_spec],
        out_specs=pl.BlockSpec((tq, hb), lambda b, h, i: (b * nq + i, h)),
        out_shape=jax.ShapeDtypeStruct((t, A_HEADS * A_V_DIM), BF16),
        scratch_shapes=[pltpu.VMEM((2, tq, LANES), F32), pltpu.VMEM((2, tq, LANES), F32),
                        pltpu.VMEM((2, tq, A_V_DIM), F32)],
        compiler_params=_params(("arbitrary", "arbitrary", "arbitrary"), vmem),
        name="diff_attention",
    )(proj, proj, proj, proj, proj_meta, proj_meta, mbias, *lams, subln_gain.reshape(1, A_V_DIM))


def _log_sigmoid(x):
    return jnp.minimum(x, 0.0) - jnp.log(1.0 + jnp.exp(-jnp.abs(x)))


def _ret_kernel(q_ref, k_ref, v_ref, g_ref, mk_ref, mv_ref, decf_ref, decb_ref, gain_ref, o_ref, fwd_sc):
    c = RET_CHUNK
    n_chunks = q_ref.shape[0] // c
    nt_dims = (((1,), (1,)), ((), ()))
    lg_f = _log_sigmoid(decf_ref[0])
    lg_b = _log_sigmoid(decb_ref[0])
    lg_f_k, lg_b_k = lg_f[:, :1], lg_b[:, :1]

    row = lax.broadcasted_iota(jnp.int32, (c, c), 0)
    col = lax.broadcasted_iota(jnp.int32, (c, c), 1)
    rel = (row - col).astype(F32)
    fwd_mask = rel >= 0
    bwd_mask = rel < 0
    d_f = jnp.where(fwd_mask, jnp.exp(jnp.where(fwd_mask, rel, 0.0) * lg_f_k), 0.0)
    d_b = jnp.where(bwd_mask, jnp.exp(jnp.where(bwd_mask, -rel, 0.0) * lg_b_k), 0.0)
    r_v = lax.broadcasted_iota(jnp.int32, (c, R_V_DIM), 0).astype(F32)
    r_k = lax.broadcasted_iota(jnp.int32, (c, R_QK_DIM), 0).astype(F32)
    qd_f = jnp.exp((r_v + 1.0) * lg_f)
    kd_f = jnp.exp((c - 1.0 - r_k) * lg_f_k)
    cd_f = jnp.exp(c * lg_f)
    qd_b = jnp.exp((c - r_v) * lg_b)
    kd_b = jnp.exp(r_k * lg_b_k)
    cd_b = jnp.exp(c * lg_b)

    def state_update(k_blk, v_blk, kd):
        kdec = (k_blk.astype(F32) * kd).T.astype(BF16)
        return jnp.dot(kdec, v_blk, preferred_element_type=F32)

    def chunk_out(q_blk, k_blk, v_blk, state, d_intra, qd):
        s = lax.dot_general(q_blk, k_blk, nt_dims, preferred_element_type=F32) * d_intra
        intra = jnp.dot(s.astype(BF16), v_blk, preferred_element_type=F32)
        inter = jnp.dot(q_blk, state.astype(BF16), preferred_element_type=F32) * qd
        return intra + inter

    r_m = lax.broadcasted_iota(jnp.int32, (CHUNK, R_QK_DIM), 0).astype(F32)
    state0 = state_update(mk_ref[...], mv_ref[...], jnp.exp((CHUNK - 1.0 - r_m) * lg_f_k))

    def fwd_body(i, state):
        off = pl.multiple_of(i * c, c)
        q_blk, k_blk, v_blk = q_ref[pl.ds(off, c), :], k_ref[pl.ds(off, c), :], v_ref[pl.ds(off, c), :]
        fwd_sc[pl.ds(off, c), :] = chunk_out(q_blk, k_blk, v_blk, state, d_f, qd_f)
        return state * cd_f + state_update(k_blk, v_blk, kd_f)

    lax.fori_loop(0, n_chunks, fwd_body, state0, unroll=RET_UNROLL)

    gain = gain_ref[0]

    def bwd_body(i, state):
        off = pl.multiple_of((n_chunks - 1 - i) * c, c)
        q_blk, k_blk, v_blk = q_ref[pl.ds(off, c), :], k_ref[pl.ds(off, c), :], v_ref[pl.ds(off, c), :]
        ro = fwd_sc[pl.ds(off, c), :] + chunk_out(q_blk, k_blk, v_blk, state, d_b, qd_b)
        ms = jnp.mean(ro * ro, axis=-1, keepdims=True)
        y = (ro * lax.rsqrt(ms + NORM_EPS)) * gain
        o_ref[pl.ds(off, c), :] = (y * _silu(g_ref[pl.ds(off, c), :].astype(F32))).astype(o_ref.dtype)
        return state * cd_b + state_update(k_blk, v_blk, kd_b)

    lax.fori_loop(0, n_chunks, bwd_body, jnp.zeros((R_QK_DIM, R_V_DIM), F32), unroll=RET_UNROLL)


def _retention(proj, proj_meta, dec_f, dec_b, ret_gain, batch, seq):
    t = batch * seq
    kb, vb = R_QK_DIM, R_V_DIM
    q_spec = pl.BlockSpec((seq, kb), lambda b, h: (b, RQ_OFF // kb + h))
    k_spec = pl.BlockSpec((seq, kb), lambda b, h: (b, RK_OFF // kb + h))
    v_spec = pl.BlockSpec((seq, vb), lambda b, h: (b, RV_OFF // vb + h))
    g_spec = pl.BlockSpec((seq, vb), lambda b, h: (b, RG_OFF // vb + h))
    mk_spec = pl.BlockSpec((CHUNK, kb), lambda b, h: (0, RK_OFF // kb + h))
    mv_spec = pl.BlockSpec((CHUNK, vb), lambda b, h: (0, RV_OFF // vb + h))
    head_spec = pl.BlockSpec((1, 1, vb), lambda b, h: (h, 0, 0))
    vmem = 2 * seq * (2 * kb + 3 * vb) * 2 + seq * vb * 4 + 64 * RET_CHUNK * vb * 4
    return pl.pallas_call(
        _ret_kernel,
        grid=(batch, R_HEADS),
        in_specs=[q_spec, k_spec, v_spec, g_spec, mk_spec, mv_spec, head_spec, head_spec, head_spec],
        out_specs=pl.BlockSpec((seq, vb), lambda b, h: (b, h)),
        out_shape=jax.ShapeDtypeStruct((t, R_HEADS * R_V_DIM), BF16),
        scratch_shapes=[pltpu.VMEM((seq, vb), F32)],
        compiler_params=_params(("arbitrary", "arbitrary"), vmem),
        name="retention",
    )(proj, proj, proj, proj, proj_meta, proj_meta, dec_f, dec_b, ret_gain)


def _out_kernel(a_ref, r_ref, w_ref, x_ref, g_ref, o_ref):
    half = a_ref.shape[1]
    h = x_ref[...] + jnp.dot(a_ref[...], w_ref[pl.ds(0, half), :], preferred_element_type=F32)
    h = h + jnp.dot(r_ref[...], w_ref[pl.ds(half, half), :], preferred_element_type=F32)
    ms = jnp.mean(h * h, axis=-1, keepdims=True)
    o_ref[...] = (h * lax.rsqrt(ms + NORM_EPS)) * g_ref[...]


def _out_proj(ao, ro, w_bf16, x2d, gain, tm):
    t, d = x2d.shape
    tm = min(tm, t)
    half = d // 2
    vmem = d * d * 2 + 2 * 2 * tm * half * 2 + 2 * 2 * tm * d * 4 + 2 * tm * d * 4
    return pl.pallas_call(
        _out_kernel,
        grid=(t // tm,),
        in_specs=[
            pl.BlockSpec((tm, half), lambda i: (i, 0)),
            pl.BlockSpec((tm, half), lambda i: (i, 0)),
            pl.BlockSpec((d, d), lambda i: (0, 0), pipeline_mode=pl.Buffered(1)),
            pl.BlockSpec((tm, d), lambda i: (i, 0)),
            pl.BlockSpec((1, d), lambda i: (0, 0)),
        ],
        out_specs=pl.BlockSpec((tm, d), lambda i: (i, 0)),
        out_shape=jax.ShapeDtypeStruct((t, d), F32),
        compiler_params=_params(("arbitrary",), vmem),
        name="out_proj",
    )(ao, ro, w_bf16, x2d, gain.reshape(1, d))


def _trunk(x, shared):
    batch, seq, d = x.shape
    x2d = x.reshape(batch * seq, d)
    u = _rmsnorm(x2d, shared["norm_gain"], NORM_TM)
    tables = _rotary_tables(jnp.arange(seq, dtype=jnp.int32) + N_META)
    proj = _proj(u, shared["w_in"], tables, seq, PROJ_TM, PROJ_TN)
    ao = _attention(proj, shared["proj_meta"], shared["mbias"], shared["lams"], shared["subln_gain"],
                    batch, seq, ATTN_TQ, ATTN_TK)
    ro = _retention(proj, shared["proj_meta"], shared["dec_f"], shared["dec_b"], shared["ret_gain"], batch, seq)
    y = _out_proj(ao, ro, shared["w_out"], x2d, shared["final_gain"], OUT_TM)
    return y.reshape(batch, seq, d)


def kernel(x_prompt, x_sample, meta_tokens, norm_gain, w_in, lambda_q1, lambda_k1, lambda_q2, lambda_k2,
           attn_subln_gain, ret_decay_fwd, ret_decay_bwd, ret_norm_gain, w_out, final_norm_gain):
    assert norm_gain.shape[0] == 1, "single-layer trunk"
    d = x_prompt.shape[-1]
    w_in_bf16 = w_in[0].astype(BF16)
    w_out_bf16 = w_out[0].astype(BF16)

    meta_pos = jnp.arange(CHUNK, dtype=jnp.int32) - (CHUNK - N_META)
    meta_rows = jnp.concatenate([jnp.zeros((CHUNK - N_META, d), F32), meta_tokens.astype(F32)], axis=0)
    u_meta = _rmsnorm(meta_rows, norm_gain[0], CHUNK)
    proj_meta = _proj(u_meta, w_in_bf16, _rotary_tables(meta_pos), CHUNK, CHUNK, PROJ_TN)
    mbias = jnp.where(meta_pos >= 0, 0.0, NEG_BIG).astype(F32).reshape(1, CHUNK)

    def per_head(v):
        return jnp.broadcast_to(v.astype(F32)[:, None, None], (R_HEADS, 1, R_V_DIM))

    shared = dict(
        norm_gain=norm_gain[0], w_in=w_in_bf16, w_out=w_out_bf16, proj_meta=proj_meta, mbias=mbias,
        lams=tuple(v[0].astype(F32).reshape(1, A_QK_DIM) for v in (lambda_q1, lambda_k1, lambda_q2, lambda_k2)),
        subln_gain=attn_subln_gain[0].astype(F32),
        dec_f=per_head(ret_decay_fwd[0]), dec_b=per_head(ret_decay_bwd[0]),
        ret_gain=ret_norm_gain[0].astype(F32).reshape(R_HEADS, 1, R_V_DIM),
        final_gain=final_norm_gain.astype(F32),
    )
    return (_trunk(x_prompt, shared), _trunk(x_sample, shared))
```

```python
import functools
import math

import jax
import jax.numpy as jnp
import numpy as np
from jax import lax
from jax.experimental import pallas as pl
from jax.experimental.pallas import tpu as pltpu

F32 = jnp.float32
BF16 = jnp.bfloat16

D_MODEL = 4096
N_META = 16
CHUNK = 128
A_HEADS = 8
A_QK_DIM = 128
A_V_DIM = 256
A_ROT_DIM = A_QK_DIM // 4
ROPE_THETA = 500000.0
R_HEADS = 8
R_QK_DIM = 128
R_V_DIM = 256
RET_THETA = 10000.0
NORM_EPS = 1e-6
NEG_BIG = -1e30
LAMBDA_INIT = 0.8 - 0.6 * math.exp(-0.3 * 0)

AQ_OFF, AK_OFF, AV_OFF, AG_OFF = 0, 2048, 4096, 6144
RQ_OFF, RK_OFF, RV_OFF, RG_OFF = 8192, 9216, 10240, 12288
PROJ_OUT = 14336

LANES = 128
LOG2E = math.log2(math.e)
ATTN_SCORE_SCALE = (A_QK_DIM ** -0.5) * LOG2E
V7X_VMEM_BYTES = 64 * 1024 * 1024
V7X_VMEM_BUDGET = 60 * 1024 * 1024

PROJ_TM = 1024
PROJ_TN = 1024
PROJ_RM = 256
ATTN_TQ = 512
ATTN_TK = 512
ATTN_RT = 128
ATTN_UNROLL = 5
OUT_TM = 256
RET_CHUNK = 256
RET_UNROLL = 4
NORM_TM = 512


def _params(sem, vmem_bytes):
    return pltpu.CompilerParams(
        dimension_semantics=sem, vmem_limit_bytes=min(int(vmem_bytes), V7X_VMEM_BUDGET)
    )


def _rmsnorm_kernel(x_ref, g_ref, o_ref):
    x = x_ref[...]
    ms = jnp.mean(x * x, axis=-1, keepdims=True)
    y = x * lax.rsqrt(ms + NORM_EPS)
    o_ref[...] = (y * g_ref[...]).astype(o_ref.dtype)


def _rmsnorm(x2d, gain, tm):
    t, d = x2d.shape
    tm = min(tm, t)
    return pl.pallas_call(
        _rmsnorm_kernel,
        grid=(t // tm,),
        in_specs=[pl.BlockSpec((tm, d), lambda i: (i, 0)), pl.BlockSpec((1, d), lambda i: (0, 0))],
        out_specs=pl.BlockSpec((tm, d), lambda i: (i, 0)),
        out_shape=jax.ShapeDtypeStruct((t, d), BF16),
        compiler_params=_params(("arbitrary",), 2 * tm * d * (4 + 2) + 4 * tm * d * 4),
        name="rmsnorm_in",
    )(x2d, gain.reshape(1, d))


def _proj_plain_kernel(u_ref, w_ref, o_ref, *, rm):
    for c in range(u_ref.shape[0] // rm):
        rows = pl.ds(c * rm, rm)
        o_ref[rows, :] = jnp.dot(u_ref[rows, :], w_ref[...], preferred_element_type=F32).astype(o_ref.dtype)


def _proj_rot_kernel(u_ref, w_ref, cos_ref, sin1_ref, sin2_ref, mult_ref, o_ref, *, rm, shift1, shift2):
    for c in range(u_ref.shape[0] // rm):
        rows = pl.ds(c * rm, rm)
        acc = jnp.dot(u_ref[rows, :], w_ref[...], preferred_element_type=F32)
        cos, sin1, sin2, mult = cos_ref[rows, :], sin1_ref[rows, :], sin2_ref[rows, :], mult_ref[rows, :]
        for g in range(acc.shape[1] // LANES):
            x = acc[:, g * LANES:(g + 1) * LANES]
            y = x * cos + pltpu.roll(x, shift1, 1) * sin1 + pltpu.roll(x, shift2, 1) * sin2
            o_ref[rows, g * LANES:(g + 1) * LANES] = (y * mult).astype(o_ref.dtype)


def _proj_plain(u, w_bf16, tm, tn, rm):
    t, d = u.shape
    tm = min(tm, t)
    n_a = (RQ_OFF - AV_OFF) // tn
    n_cols = (RQ_OFF - AV_OFF) + (PROJ_OUT - RV_OFF)
    vmem = 2 * tm * d * 2 + 2 * d * tn * 2 + 2 * tm * tn * 2 + 2 * min(rm, tm) * tn * 4
    return pl.pallas_call(
        functools.partial(_proj_plain_kernel, rm=min(rm, tm)),
        grid=(t // tm, n_cols // tn),
        in_specs=[pl.BlockSpec((tm, d), lambda i, j: (i, 0)),
                  pl.BlockSpec((d, tn), lambda i, j: (0, jnp.where(j < n_a, AV_OFF // tn + j, RV_OFF // tn + j - n_a)))],
        out_specs=pl.BlockSpec((tm, tn), lambda i, j: (i, j)),
        out_shape=jax.ShapeDtypeStruct((t, n_cols), BF16),
        compiler_params=_params(("arbitrary", "arbitrary"), vmem),
        name="in_proj_plain",
    )(u, w_bf16)


def _proj_rot(u, w_bf16, tables, mults, col_off, n_cols, shifts, rows_per_seq, tm, tn, rm, name):
    t, d = u.shape
    tm = min(tm, rows_per_seq)
    n_seq_tiles = rows_per_seq // tm
    n_j = n_cols // tn
    tab_spec = pl.BlockSpec((tm, LANES), lambda i, j: (i % n_seq_tiles, 0))
    mult_spec = pl.BlockSpec((None, tm, LANES), lambda i, j: (j // (n_j // 2), i % n_seq_tiles, 0))
    vmem = 2 * tm * d * 2 + 2 * d * tn * 2 + 2 * tm * tn * 2 + 2 * 4 * tm * LANES * 4 + 3 * min(rm, tm) * tn * 4
    return pl.pallas_call(
        functools.partial(_proj_rot_kernel, rm=min(rm, tm), shift1=shifts[0], shift2=shifts[1]),
        grid=(t // tm, n_j),
        in_specs=[pl.BlockSpec((tm, d), lambda i, j: (i, 0)),
                  pl.BlockSpec((d, tn), lambda i, j: (0, col_off // tn + j)),
                  tab_spec, tab_spec, tab_spec, mult_spec],
        out_specs=pl.BlockSpec((tm, tn), lambda i, j: (i, j)),
        out_shape=jax.ShapeDtypeStruct((t, n_cols), BF16),
        compiler_params=_params(("arbitrary", "arbitrary"), vmem),
        name=name,
    )(u, w_bf16, *tables, mults)


def _project(u, w_bf16, tabs, rows_per_seq):
    proj_a = _proj_rot(u, w_bf16, tabs["attn"], tabs["attn_mult"], AQ_OFF, AV_OFF - AQ_OFF,
                       (LANES - A_ROT_DIM // 2, A_ROT_DIM // 2), rows_per_seq, PROJ_TM, PROJ_TN, PROJ_RM, "in_proj_attn")
    proj_p = _proj_plain(u, w_bf16, PROJ_TM, PROJ_TN, PROJ_RM)
    proj_r = _proj_rot(u, w_bf16, tabs["ret"], tabs["ret_mult"], RQ_OFF, RV_OFF - RQ_OFF,
                       (R_QK_DIM // 2, R_QK_DIM // 2), rows_per_seq, PROJ_TM, PROJ_TN, PROJ_RM, "in_proj_ret")
    return proj_a, proj_p, proj_r


def _rotary_tables(pos):
    n = pos.shape[0]
    posf = pos.astype(F32)[:, None]
    half = A_ROT_DIM // 2
    inv = jnp.power(jnp.float32(ROPE_THETA), -jnp.arange(half, dtype=F32) * 2.0 / A_ROT_DIM)
    ang = posf * inv[None, :]
    cos, sin = jnp.cos(ang), jnp.sin(ang)
    ones = jnp.ones((n, LANES - A_ROT_DIM), F32)
    zeros_h = jnp.zeros((n, half), F32)
    zeros_r = jnp.zeros((n, LANES - A_ROT_DIM), F32)
    cos_a = jnp.concatenate([cos, cos, ones], axis=1)
    sin_hi = jnp.concatenate([-sin, zeros_h, zeros_r], axis=1)
    sin_lo = jnp.concatenate([zeros_h, sin, zeros_r], axis=1)
    half_r = R_QK_DIM // 2
    inv_r = jnp.power(jnp.float32(RET_THETA), -jnp.arange(half_r, dtype=F32) * 2.0 / R_QK_DIM)
    ang_r = posf * inv_r[None, :]
    cos_r = jnp.concatenate([jnp.cos(ang_r), jnp.cos(ang_r)], axis=1)
    sin_r = jnp.concatenate([-jnp.sin(ang_r), jnp.sin(ang_r)], axis=1)
    one = jnp.ones((n, LANES), F32)
    kmask = jnp.broadcast_to((pos >= 0).astype(F32)[:, None], (n, LANES))
    return dict(
        attn=(cos_a, sin_hi, sin_lo),
        attn_mult=jnp.stack([one * ATTN_SCORE_SCALE, one]),
        ret=(cos_r, sin_r, jnp.zeros((n, LANES), F32)),
        ret_mult=jnp.stack([one, (R_QK_DIM ** -0.5) * kmask]),
    )


def _silu(g):
    return g * (1.0 / (1.0 + jnp.exp(-g)))


def _attn_kernel(q_ref, k_ref, v_ref, g_ref, mk_ref, mv_ref, mbias_ref, lq1_ref, lk1_ref, lq2_ref, lk2_ref,
                 gain_ref, o_ref, m_sc, l_sc, acc_sc, *, tk, rt):
    tq = q_ref.shape[0]
    n_kb = k_ref.shape[0] // tk
    nt_dims = (((1,), (1,)), ((), ()))
    sub = (pl.ds(0, A_QK_DIM), pl.ds(A_QK_DIM, A_QK_DIM))

    def lane_tiles(x):
        return [x[:, t * LANES:(t + 1) * LANES] for t in range(x.shape[1] // LANES)]

    def update(idx, rows, s, v_blk, first):
        tiles = lane_tiles(s)
        mx = functools.reduce(jnp.maximum, tiles)
        m_cur = jnp.max(mx, axis=-1, keepdims=True)
        if first:
            m_new = jnp.broadcast_to(m_cur, (rt, LANES))
        else:
            m_prev = m_sc[idx, rows, :]
            m_new = jnp.maximum(m_prev, m_cur)
        p_tiles = [jnp.exp2(t - m_new) for t in tiles]
        p = jnp.concatenate(p_tiles, axis=1) if len(p_tiles) > 1 else p_tiles[0]
        pv = jnp.dot(p.astype(BF16), v_blk, preferred_element_type=F32)
        l_part = functools.reduce(jnp.add, p_tiles)
        if first:
            l_sc[idx, rows, :] = l_part
            acc_sc[idx, rows, :] = pv
        else:
            alpha = jnp.exp2(m_prev - m_new)
            l_sc[idx, rows, :] = alpha * l_sc[idx, rows, :] + l_part
            acc_sc[idx, rows, :] = jnp.concatenate([alpha, alpha], axis=1) * acc_sc[idx, rows, :] + pv
        m_sc[idx, rows, :] = m_new

    def scores(rows, idx, keys):
        return lax.dot_general(q_ref[rows, sub[idx]], keys, nt_dims, preferred_element_type=F32)

    mbias = mbias_ref[...] * LOG2E
    v_first = jnp.concatenate([mv_ref[...], v_ref[pl.ds(0, tk), :]], axis=0)
    for r in range(tq // rt):
        rows = pl.ds(r * rt, rt)
        for idx in range(2):
            s_meta = scores(rows, idx, mk_ref[:, sub[idx]]) + mbias
            s_tok = scores(rows, idx, k_ref[pl.ds(0, tk), sub[idx]])
            update(idx, rows, jnp.concatenate([s_meta, s_tok], axis=1), v_first, True)

    def body(kb, carry):
        off = pl.multiple_of(kb * tk, tk)
        v_blk = v_ref[pl.ds(off, tk), :]
        for r in range(tq // rt):
            rows = pl.ds(r * rt, rt)
            for idx in range(2):
                update(idx, rows, scores(rows, idx, k_ref[pl.ds(off, tk), sub[idx]]), v_blk, False)
        return carry

    lax.fori_loop(1, n_kb, body, 0, unroll=ATTN_UNROLL)

    lam = (jnp.exp(jnp.sum(lq1_ref[...] * lk1_ref[...], axis=-1, keepdims=True))
           - jnp.exp(jnp.sum(lq2_ref[...] * lk2_ref[...], axis=-1, keepdims=True)) + LAMBDA_INIT)
    l1 = jnp.sum(l_sc[0], axis=-1, keepdims=True)
    l2 = jnp.sum(l_sc[1], axis=-1, keepdims=True)
    o = acc_sc[0] / l1 - lam * (acc_sc[1] / l2)
    ms = jnp.mean(o * o, axis=-1, keepdims=True)
    y = (o * lax.rsqrt(ms + NORM_EPS)) * gain_ref[...]
    y = y * (1.0 - LAMBDA_INIT)
    o_ref[...] = (y * _silu(g_ref[...].astype(F32))).astype(o_ref.dtype)


def _attention(projs, metas, mbias, lams, subln_gain, batch, seq, tq, tk):
    t = batch * seq
    tq, tk = min(tq, seq), min(tk, seq)
    nq = seq // tq
    hb = A_V_DIM
    proj_a, proj_p, _ = projs
    meta_a, meta_p, _ = metas
    q_spec = pl.BlockSpec((tq, hb), lambda b, h, i: (b * nq + i, h))
    k_spec = pl.BlockSpec((seq, hb), lambda b, h, i: (b, A_HEADS + h))
    v_spec = pl.BlockSpec((seq, hb), lambda b, h, i: (b, h))
    g_spec = pl.BlockSpec((tq, hb), lambda b, h, i: (b * nq + i, A_HEADS + h))
    mk_spec = pl.BlockSpec((CHUNK, hb), lambda b, h, i: (0, A_HEADS + h))
    mv_spec = pl.BlockSpec((CHUNK, hb), lambda b, h, i: (0, h))
    row128 = pl.BlockSpec((1, LANES), lambda b, h, i: (0, 0))
    gain_spec = pl.BlockSpec((1, A_V_DIM), lambda b, h, i: (0, 0))
    vmem = (2 * 2 * seq * hb * 2 + 2 * 3 * tq * hb * 2 + 2 * 2 * CHUNK * hb * 2
            + 2 * tq * (2 * LANES + A_V_DIM) * 4 + 6 * tq * tk * 4)
    return pl.pallas_call(
        functools.partial(_attn_kernel, tk=tk, rt=min(ATTN_RT, tq)),
        grid=(batch, A_HEADS, nq),
        in_specs=[q_spec, k_spec, v_spec, g_spec, mk_spec, mv_spec, row128, row128, row128, row128, row128,
                  gain_spec],
        out_specs=pl.BlockSpec((tq, hb), lambda b, h, i: (b * nq + i, h)),
        out_shape=jax.ShapeDtypeStruct((t, A_HEADS * A_V_DIM), BF16),
        scratch_shapes=[pltpu.VMEM((2, tq, LANES), F32), pltpu.VMEM((2, tq, LANES), F32),
                        pltpu.VMEM((2, tq, A_V_DIM), F32)],
        compiler_params=_params(("arbitrary", "arbitrary", "arbitrary"), vmem),
        name="diff_attention",
    )(proj_a, proj_a, proj_p, proj_p, meta_a, meta_p, mbias, *lams, subln_gain.reshape(1, A_V_DIM))


def _log_sigmoid(x):
    return jnp.minimum(x, 0.0) - jnp.log(1.0 + jnp.exp(-jnp.abs(x)))


def _ret_kernel(q_ref, k_ref, v_ref, g_ref, mk_ref, mv_ref, decf_ref, decb_ref, gain_ref, o_ref, fwd_sc):
    c = RET_CHUNK
    n_chunks = q_ref.shape[0] // c
    nt_dims = (((1,), (1,)), ((), ()))
    lg_f = _log_sigmoid(decf_ref[0])
    lg_b = _log_sigmoid(decb_ref[0])
    lg_f_k, lg_b_k = lg_f[:, :1], lg_b[:, :1]

    row = lax.broadcasted_iota(jnp.int32, (c, c), 0)
    col = lax.broadcasted_iota(jnp.int32, (c, c), 1)
    rel = (row - col).astype(F32)
    fwd_mask = rel >= 0
    bwd_mask = rel < 0
    d_f = jnp.where(fwd_mask, jnp.exp(jnp.where(fwd_mask, rel, 0.0) * lg_f_k), 0.0)
    d_b = jnp.where(bwd_mask, jnp.exp(jnp.where(bwd_mask, -rel, 0.0) * lg_b_k), 0.0)
    r_v = lax.broadcasted_iota(jnp.int32, (c, R_V_DIM), 0).astype(F32)
    r_k = lax.broadcasted_iota(jnp.int32, (c, R_QK_DIM), 0).astype(F32)
    qd_f = jnp.exp((r_v + 1.0) * lg_f)
    kd_f = jnp.exp((c - 1.0 - r_k) * lg_f_k)
    cd_f = jnp.exp(c * lg_f)
    qd_b = jnp.exp((c - r_v) * lg_b)
    kd_b = jnp.exp(r_k * lg_b_k)
    cd_b = jnp.exp(c * lg_b)

    def state_update(k_blk, v_blk, kd):
        kdec = (k_blk.astype(F32) * kd).T.astype(BF16)
        return jnp.dot(kdec, v_blk, preferred_element_type=F32)

    def chunk_out(q_blk, k_blk, v_blk, state, d_intra, qd):
        s = lax.dot_general(q_blk, k_blk, nt_dims, preferred_element_type=F32) * d_intra
        intra = jnp.dot(s.astype(BF16), v_blk, preferred_element_type=F32)
        inter = jnp.dot(q_blk, state.astype(BF16), preferred_element_type=F32) * qd
        return intra + inter

    r_m = lax.broadcasted_iota(jnp.int32, (CHUNK, R_QK_DIM), 0).astype(F32)
    state0 = state_update(mk_ref[...], mv_ref[...], jnp.exp((CHUNK - 1.0 - r_m) * lg_f_k))

    def fwd_body(i, state):
        off = pl.multiple_of(i * c, c)
        q_blk, k_blk, v_blk = q_ref[pl.ds(off, c), :], k_ref[pl.ds(off, c), :], v_ref[pl.ds(off, c), :]
        fwd_sc[pl.ds(off, c), :] = chunk_out(q_blk, k_blk, v_blk, state, d_f, qd_f)
        return state * cd_f + state_update(k_blk, v_blk, kd_f)

    lax.fori_loop(0, n_chunks, fwd_body, state0, unroll=RET_UNROLL)

    gain = gain_ref[0]

    def bwd_body(i, state):
        off = pl.multiple_of((n_chunks - 1 - i) * c, c)
        q_blk, k_blk, v_blk = q_ref[pl.ds(off, c), :], k_ref[pl.ds(off, c), :], v_ref[pl.ds(off, c), :]
        ro = fwd_sc[pl.ds(off, c), :] + chunk_out(q_blk, k_blk, v_blk, state, d_b, qd_b)
        ms = jnp.mean(ro * ro, axis=-1, keepdims=True)
        y = (ro * lax.rsqrt(ms + NORM_EPS)) * gain
        o_ref[pl.ds(off, c), :] = (y * _silu(g_ref[pl.ds(off, c), :].astype(F32))).astype(o_ref.dtype)
        return state * cd_b + state_update(k_blk, v_blk, kd_b)

    lax.fori_loop(0, n_chunks, bwd_body, jnp.zeros((R_QK_DIM, R_V_DIM), F32), unroll=RET_UNROLL)


def _retention(projs, metas, dec_f, dec_b, ret_gain, batch, seq):
    t = batch * seq
    kb, vb = R_QK_DIM, R_V_DIM
    _, proj_p, proj_r = projs
    _, meta_p, meta_r = metas
    q_spec = pl.BlockSpec((seq, kb), lambda b, h: (b, h))
    k_spec = pl.BlockSpec((seq, kb), lambda b, h: (b, R_HEADS + h))
    v_spec = pl.BlockSpec((seq, vb), lambda b, h: (b, 2 * A_HEADS + h))
    g_spec = pl.BlockSpec((seq, vb), lambda b, h: (b, 2 * A_HEADS + R_HEADS + h))
    mk_spec = pl.BlockSpec((CHUNK, kb), lambda b, h: (0, R_HEADS + h))
    mv_spec = pl.BlockSpec((CHUNK, vb), lambda b, h: (0, 2 * A_HEADS + h))
    head_spec = pl.BlockSpec((1, 1, vb), lambda b, h: (h, 0, 0))
    vmem = 2 * seq * (2 * kb + 3 * vb) * 2 + seq * vb * 4 + 64 * RET_CHUNK * vb * 4
    return pl.pallas_call(
        _ret_kernel,
        grid=(batch, R_HEADS),
        in_specs=[q_spec, k_spec, v_spec, g_spec, mk_spec, mv_spec, head_spec, head_spec, head_spec],
        out_specs=pl.BlockSpec((seq, vb), lambda b, h: (b, h)),
        out_shape=jax.ShapeDtypeStruct((t, R_HEADS * R_V_DIM), BF16),
        scratch_shapes=[pltpu.VMEM((seq, vb), F32)],
        compiler_params=_params(("arbitrary", "arbitrary"), vmem),
        name="retention",
    )(proj_r, proj_r, proj_p, proj_p, meta_r, meta_p, dec_f, dec_b, ret_gain)


def _out_kernel(a_ref, r_ref, w_ref, x_ref, g_ref, o_ref):
    half = a_ref.shape[1]
    h = x_ref[...] + jnp.dot(a_ref[...], w_ref[pl.ds(0, half), :], preferred_element_type=F32)
    h = h + jnp.dot(r_ref[...], w_ref[pl.ds(half, half), :], preferred_element_type=F32)
    ms = jnp.mean(h * h, axis=-1, keepdims=True)
    o_ref[...] = (h * lax.rsqrt(ms + NORM_EPS)) * g_ref[...]


def _out_proj(ao, ro, w_bf16, x2d, gain, tm):
    t, d = x2d.shape
    tm = min(tm, t)
    half = d // 2
    vmem = d * d * 2 + 2 * 2 * tm * half * 2 + 2 * 2 * tm * d * 4 + 2 * tm * d * 4
    return pl.pallas_call(
        _out_kernel,
        grid=(t // tm,),
        in_specs=[
            pl.BlockSpec((tm, half), lambda i: (i, 0)),
            pl.BlockSpec((tm, half), lambda i: (i, 0)),
            pl.BlockSpec((d, d), lambda i: (0, 0), pipeline_mode=pl.Buffered(1)),
            pl.BlockSpec((tm, d), lambda i: (i, 0)),
            pl.BlockSpec((1, d), lambda i: (0, 0)),
        ],
        out_specs=pl.BlockSpec((tm, d), lambda i: (i, 0)),
        out_shape=jax.ShapeDtypeStruct((t, d), F32),
        compiler_params=_params(("arbitrary",), vmem),
        name="out_proj",
    )(ao, ro, w_bf16, x2d, gain.reshape(1, d))


def _trunk(x, shared):
    batch, seq, d = x.shape
    x2d = x.reshape(batch * seq, d)
    u = _rmsnorm(x2d, shared["norm_gain"], NORM_TM)
    projs = _project(u, shared["w_in"], _rotary_tables(jnp.arange(seq, dtype=jnp.int32) + N_META), seq)
    ao = _attention(projs, shared["proj_meta"], shared["mbias"], shared["lams"], shared["subln_gain"],
                    batch, seq, ATTN_TQ, ATTN_TK)
    ro = _retention(projs, shared["proj_meta"], shared["dec_f"], shared["dec_b"], shared["ret_gain"], batch, seq)
    y = _out_proj(ao, ro, shared["w_out"], x2d, shared["final_gain"], OUT_TM)
    return y.reshape(batch, seq, d)


def kernel(x_prompt, x_sample, meta_tokens, norm_gain, w_in, lambda_q1, lambda_k1, lambda_q2, lambda_k2,
           attn_subln_gain, ret_decay_fwd, ret_decay_bwd, ret_norm_gain, w_out, final_norm_gain):
    assert norm_gain.shape[0] == 1, "single-layer trunk"
    d = x_prompt.shape[-1]
    w_in_bf16 = w_in[0].astype(BF16)
    w_out_bf16 = w_out[0].astype(BF16)

    meta_pos = jnp.arange(CHUNK, dtype=jnp.int32) - (CHUNK - N_META)
    meta_rows = jnp.concatenate([jnp.zeros((CHUNK - N_META, d), F32), meta_tokens.astype(F32)], axis=0)
    u_meta = _rmsnorm(meta_rows, norm_gain[0], CHUNK)
    proj_meta = _project(u_meta, w_in_bf16, _rotary_tables(meta_pos), CHUNK)
    mbias = jnp.where(meta_pos >= 0, 0.0, NEG_BIG).astype(F32).reshape(1, CHUNK)

    def per_head(v):
        return jnp.broadcast_to(v.astype(F32)[:, None, None], (R_HEADS, 1, R_V_DIM))

    shared = dict(
        norm_gain=norm_gain[0], w_in=w_in_bf16, w_out=w_out_bf16, proj_meta=proj_meta, mbias=mbias,
        lams=tuple(v[0].astype(F32).reshape(1, A_QK_DIM) for v in (lambda_q1, lambda_k1, lambda_q2, lambda_k2)),
        subln_gain=attn_subln_gain[0].astype(F32),
        dec_f=per_head(ret_decay_fwd[0]), dec_b=per_head(ret_decay_bwd[0]),
        ret_gain=ret_norm_gain[0].astype(F32).reshape(R_HEADS, 1, R_V_DIM),
        final_gain=final_norm_gain.astype(F32),
    )
    return (_trunk(x_prompt, shared), _trunk(x_sample, shared))
```

```python
import functools
import math

import jax
import jax.numpy as jnp
import numpy as np
from jax import lax
from jax.experimental import pallas as pl
from jax.experimental.pallas import tpu as pltpu

F32 = jnp.float32
BF16 = jnp.bfloat16

D_MODEL = 4096
N_META = 16
CHUNK = 128
A_HEADS = 8
A_QK_DIM = 128
A_V_DIM = 256
A_ROT_DIM = A_QK_DIM // 4
ROPE_THETA = 500000.0
R_HEADS = 8
R_QK_DIM = 128
R_V_DIM = 256
RET_THETA = 10000.0
NORM_EPS = 1e-6
NEG_BIG = -1e30
LAMBDA_INIT = 0.8 - 0.6 * math.exp(-0.3 * 0)

AQ_OFF, AK_OFF, AV_OFF, AG_OFF = 0, 2048, 4096, 6144
RQ_OFF, RK_OFF, RV_OFF, RG_OFF = 8192, 9216, 10240, 12288
PROJ_OUT = 14336

LANES = 128
LOG2E = math.log2(math.e)
ATTN_SCORE_SCALE = (A_QK_DIM ** -0.5) * LOG2E
V7X_VMEM_BYTES = 64 * 1024 * 1024
V7X_VMEM_BUDGET = 60 * 1024 * 1024

PROJ_TM = 1024
PROJ_TN = 1024
PROJ_RM = 256
ATTN_TQ = 1024
ATTN_TK = 512
ATTN_RT = 128
ATTN_UNROLL = 5
OUT_TM = 256
RET_CHUNK = 256
RET_UNROLL = 4
NORM_TM = 512


def _params(sem, vmem_bytes):
    return pltpu.CompilerParams(
        dimension_semantics=sem, vmem_limit_bytes=min(int(vmem_bytes), V7X_VMEM_BUDGET)
    )


def _rmsnorm_kernel(x_ref, g_ref, o_ref):
    x = x_ref[...]
    ms = jnp.mean(x * x, axis=-1, keepdims=True)
    y = x * lax.rsqrt(ms + NORM_EPS)
    o_ref[...] = (y * g_ref[...]).astype(o_ref.dtype)


def _rmsnorm(x2d, gain, tm):
    t, d = x2d.shape
    tm = min(tm, t)
    return pl.pallas_call(
        _rmsnorm_kernel,
        grid=(t // tm,),
        in_specs=[pl.BlockSpec((tm, d), lambda i: (i, 0)), pl.BlockSpec((1, d), lambda i: (0, 0))],
        out_specs=pl.BlockSpec((tm, d), lambda i: (i, 0)),
        out_shape=jax.ShapeDtypeStruct((t, d), BF16),
        compiler_params=_params(("arbitrary",), 2 * tm * d * (4 + 2) + 4 * tm * d * 4),
        name="rmsnorm_in",
    )(x2d, gain.reshape(1, d))


def _proj_plain_kernel(u_ref, w_ref, o_ref, *, rm):
    for c in range(u_ref.shape[0] // rm):
        rows = pl.ds(c * rm, rm)
        o_ref[rows, :] = jnp.dot(u_ref[rows, :], w_ref[...], preferred_element_type=F32).astype(o_ref.dtype)


def _proj_rot_kernel(u_ref, w_ref, cos_ref, sin1_ref, sin2_ref, mult_ref, o_ref, *, rm, shift1, shift2):
    for c in range(u_ref.shape[0] // rm):
        rows = pl.ds(c * rm, rm)
        acc = jnp.dot(u_ref[rows, :], w_ref[...], preferred_element_type=F32)
        cos, sin1, sin2, mult = cos_ref[rows, :], sin1_ref[rows, :], sin2_ref[rows, :], mult_ref[rows, :]
        for g in range(acc.shape[1] // LANES):
            x = acc[:, g * LANES:(g + 1) * LANES]
            y = x * cos + pltpu.roll(x, shift1, 1) * sin1 + pltpu.roll(x, shift2, 1) * sin2
            o_ref[rows, g * LANES:(g + 1) * LANES] = (y * mult).astype(o_ref.dtype)


def _proj_plain(u, w_bf16, tm, tn, rm):
    t, d = u.shape
    tm = min(tm, t)
    n_a = (RQ_OFF - AV_OFF) // tn
    n_cols = (RQ_OFF - AV_OFF) + (PROJ_OUT - RV_OFF)
    vmem = 2 * tm * d * 2 + 2 * d * tn * 2 + 2 * tm * tn * 2 + 2 * min(rm, tm) * tn * 4
    return pl.pallas_call(
        functools.partial(_proj_plain_kernel, rm=min(rm, tm)),
        grid=(t // tm, n_cols // tn),
        in_specs=[pl.BlockSpec((tm, d), lambda i, j: (i, 0)),
                  pl.BlockSpec((d, tn), lambda i, j: (0, jnp.where(j < n_a, AV_OFF // tn + j, RV_OFF // tn + j - n_a)))],
        out_specs=pl.BlockSpec((tm, tn), lambda i, j: (i, j)),
        out_shape=jax.ShapeDtypeStruct((t, n_cols), BF16),
        compiler_params=_params(("arbitrary", "arbitrary"), vmem),
        name="in_proj_plain",
    )(u, w_bf16)


def _proj_rot(u, w_bf16, tables, mults, col_off, n_cols, shifts, rows_per_seq, tm, tn, rm, name):
    t, d = u.shape
    tm = min(tm, rows_per_seq)
    n_seq_tiles = rows_per_seq // tm
    n_j = n_cols // tn
    tab_spec = pl.BlockSpec((tm, LANES), lambda i, j: (i % n_seq_tiles, 0))
    mult_spec = pl.BlockSpec((None, tm, LANES), lambda i, j: (j // (n_j // 2), i % n_seq_tiles, 0))
    vmem = 2 * tm * d * 2 + 2 * d * tn * 2 + 2 * tm * tn * 2 + 2 * 4 * tm * LANES * 4 + 3 * min(rm, tm) * tn * 4
    return pl.pallas_call(
        functools.partial(_proj_rot_kernel, rm=min(rm, tm), shift1=shifts[0], shift2=shifts[1]),
        grid=(t // tm, n_j),
        in_specs=[pl.BlockSpec((tm, d), lambda i, j: (i, 0)),
                  pl.BlockSpec((d, tn), lambda i, j: (0, col_off // tn + j)),
                  tab_spec, tab_spec, tab_spec, mult_spec],
        out_specs=pl.BlockSpec((tm, tn), lambda i, j: (i, j)),
        out_shape=jax.ShapeDtypeStruct((t, n_cols), BF16),
        compiler_params=_params(("arbitrary", "arbitrary"), vmem),
        name=name,
    )(u, w_bf16, *tables, mults)


def _project(u, w_bf16, tabs, rows_per_seq):
    proj_a = _proj_rot(u, w_bf16, tabs["attn"], tabs["attn_mult"], AQ_OFF, AV_OFF - AQ_OFF,
                       (LANES - A_ROT_DIM // 2, A_ROT_DIM // 2), rows_per_seq, PROJ_TM, PROJ_TN, PROJ_RM, "in_proj_attn")
    proj_p = _proj_plain(u, w_bf16, PROJ_TM, PROJ_TN, PROJ_RM)
    proj_r = _proj_rot(u, w_bf16, tabs["ret"], tabs["ret_mult"], RQ_OFF, RV_OFF - RQ_OFF,
                       (R_QK_DIM // 2, R_QK_DIM // 2), rows_per_seq, PROJ_TM, PROJ_TN, PROJ_RM, "in_proj_ret")
    return proj_a, proj_p, proj_r


def _rotary_tables(pos):
    n = pos.shape[0]
    posf = pos.astype(F32)[:, None]
    half = A_ROT_DIM // 2
    inv = jnp.power(jnp.float32(ROPE_THETA), -jnp.arange(half, dtype=F32) * 2.0 / A_ROT_DIM)
    ang = posf * inv[None, :]
    cos, sin = jnp.cos(ang), jnp.sin(ang)
    ones = jnp.ones((n, LANES - A_ROT_DIM), F32)
    zeros_h = jnp.zeros((n, half), F32)
    zeros_r = jnp.zeros((n, LANES - A_ROT_DIM), F32)
    cos_a = jnp.concatenate([cos, cos, ones], axis=1)
    sin_hi = jnp.concatenate([-sin, zeros_h, zeros_r], axis=1)
    sin_lo = jnp.concatenate([zeros_h, sin, zeros_r], axis=1)
    half_r = R_QK_DIM // 2
    inv_r = jnp.power(jnp.float32(RET_THETA), -jnp.arange(half_r, dtype=F32) * 2.0 / R_QK_DIM)
    ang_r = posf * inv_r[None, :]
    cos_r = jnp.concatenate([jnp.cos(ang_r), jnp.cos(ang_r)], axis=1)
    sin_r = jnp.concatenate([-jnp.sin(ang_r), jnp.sin(ang_r)], axis=1)
    one = jnp.ones((n, LANES), F32)
    kmask = jnp.broadcast_to((pos >= 0).astype(F32)[:, None], (n, LANES))
    return dict(
        attn=(cos_a, sin_hi, sin_lo),
        attn_mult=jnp.stack([one * ATTN_SCORE_SCALE, one]),
        ret=(cos_r, sin_r, jnp.zeros((n, LANES), F32)),
        ret_mult=jnp.stack([one, (R_QK_DIM ** -0.5) * kmask]),
    )


def _silu(g):
    return g * (1.0 / (1.0 + jnp.exp(-g)))


def _attn_kernel(q_ref, k_ref, v_ref, g_ref, mk_ref, mv_ref, mbias_ref, lq1_ref, lk1_ref, lq2_ref, lk2_ref,
                 gain_ref, o_ref, m_sc, l_sc, acc_sc, *, tk, rt):
    tq = q_ref.shape[0]
    n_kb = k_ref.shape[0] // tk
    nt_dims = (((1,), (1,)), ((), ()))
    sub = (pl.ds(0, A_QK_DIM), pl.ds(A_QK_DIM, A_QK_DIM))

    def lane_tiles(x):
        return [x[:, t * LANES:(t + 1) * LANES] for t in range(x.shape[1] // LANES)]

    def update(idx, rows, s, v_blk, first):
        tiles = lane_tiles(s)
        mx = functools.reduce(jnp.maximum, tiles)
        m_cur = jnp.max(mx, axis=-1, keepdims=True)
        if first:
            m_new = jnp.broadcast_to(m_cur, (rt, LANES))
        else:
            m_prev = m_sc[idx, rows, :]
            m_new = jnp.maximum(m_prev, m_cur)
        p_tiles = [jnp.exp2(t - m_new) for t in tiles]
        p = jnp.concatenate(p_tiles, axis=1) if len(p_tiles) > 1 else p_tiles[0]
        pv = jnp.dot(p.astype(BF16), v_blk, preferred_element_type=F32)
        l_part = functools.reduce(jnp.add, p_tiles)
        if first:
            l_sc[idx, rows, :] = l_part
            acc_sc[idx, rows, :] = pv
        else:
            alpha = jnp.exp2(m_prev - m_new)
            l_sc[idx, rows, :] = alpha * l_sc[idx, rows, :] + l_part
            acc_sc[idx, rows, :] = jnp.concatenate([alpha, alpha], axis=1) * acc_sc[idx, rows, :] + pv
        m_sc[idx, rows, :] = m_new

    def scores(rows, idx, keys):
        return lax.dot_general(q_ref[rows, sub[idx]], keys, nt_dims, preferred_element_type=F32)

    mbias = mbias_ref[...] * LOG2E
    v_first = jnp.concatenate([mv_ref[...], v_ref[pl.ds(0, tk), :]], axis=0)
    for r in range(tq // rt):
        rows = pl.ds(r * rt, rt)
        for idx in range(2):
            s_meta = scores(rows, idx, mk_ref[:, sub[idx]]) + mbias
            s_tok = scores(rows, idx, k_ref[pl.ds(0, tk), sub[idx]])
            update(idx, rows, jnp.concatenate([s_meta, s_tok], axis=1), v_first, True)

    def body(kb, carry):
        off = pl.multiple_of(kb * tk, tk)
        v_blk = v_ref[pl.ds(off, tk), :]
        for r in range(tq // rt):
            rows = pl.ds(r * rt, rt)
            for idx in range(2):
                update(idx, rows, scores(rows, idx, k_ref[pl.ds(off, tk), sub[idx]]), v_blk, False)
        return carry

    lax.fori_loop(1, n_kb, body, 0, unroll=ATTN_UNROLL)

    lam = (jnp.exp(jnp.sum(lq1_ref[...] * lk1_ref[...], axis=-1, keepdims=True))
           - jnp.exp(jnp.sum(lq2_ref[...] * lk2_ref[...], axis=-1, keepdims=True)) + LAMBDA_INIT)
    inv1 = 1.0 / jnp.sum(l_sc[0], axis=-1, keepdims=True)
    inv2 = lam / jnp.sum(l_sc[1], axis=-1, keepdims=True)
    o = acc_sc[0] * inv1 - acc_sc[1] * inv2
    ms = jnp.mean(o * o, axis=-1, keepdims=True)
    y = (o * lax.rsqrt(ms + NORM_EPS)) * (gain_ref[...] * (1.0 - LAMBDA_INIT))
    o_ref[...] = (y * _silu(g_ref[...].astype(F32))).astype(o_ref.dtype)


def _attention(projs, metas, mbias, lams, subln_gain, batch, seq, tq, tk):
    t = batch * seq
    tq, tk = min(tq, seq), min(tk, seq)
    nq = seq // tq
    hb = A_V_DIM
    proj_a, proj_p, _ = projs
    meta_a, meta_p, _ = metas
    q_spec = pl.BlockSpec((tq, hb), lambda b, h, i: (b * nq + i, h))
    k_spec = pl.BlockSpec((seq, hb), lambda b, h, i: (b, A_HEADS + h))
    v_spec = pl.BlockSpec((seq, hb), lambda b, h, i: (b, h))
    g_spec = pl.BlockSpec((tq, hb), lambda b, h, i: (b * nq + i, A_HEADS + h))
    mk_spec = pl.BlockSpec((CHUNK, hb), lambda b, h, i: (0, A_HEADS + h))
    mv_spec = pl.BlockSpec((CHUNK, hb), lambda b, h, i: (0, h))
    row128 = pl.BlockSpec((1, LANES), lambda b, h, i: (0, 0))
    gain_spec = pl.BlockSpec((1, A_V_DIM), lambda b, h, i: (0, 0))
    vmem = (2 * 2 * seq * hb * 2 + 2 * 3 * tq * hb * 2 + 2 * 2 * CHUNK * hb * 2
            + 2 * tq * (2 * LANES + A_V_DIM) * 4 + 6 * tq * tk * 4)
    return pl.pallas_call(
        functools.partial(_attn_kernel, tk=tk, rt=min(ATTN_RT, tq)),
        grid=(batch, A_HEADS, nq),
        in_specs=[q_spec, k_spec, v_spec, g_spec, mk_spec, mv_spec, row128, row128, row128, row128, row128,
                  gain_spec],
        out_specs=pl.BlockSpec((tq, hb), lambda b, h, i: (b * nq + i, h)),
        out_shape=jax.ShapeDtypeStruct((t, A_HEADS * A_V_DIM), BF16),
        scratch_shapes=[pltpu.VMEM((2, tq, LANES), F32), pltpu.VMEM((2, tq, LANES), F32),
                        pltpu.VMEM((2, tq, A_V_DIM), F32)],
        compiler_params=_params(("arbitrary", "arbitrary", "arbitrary"), vmem),
        name="diff_attention",
    )(proj_a, proj_a, proj_p, proj_p, meta_a, meta_p, mbias, *lams, subln_gain.reshape(1, A_V_DIM))


def _log_sigmoid(x):
    return jnp.minimum(x, 0.0) - jnp.log(1.0 + jnp.exp(-jnp.abs(x)))


def _ret_kernel(q_ref, k_ref, v_ref, g_ref, mk_ref, mv_ref, decf_ref, decb_ref, gain_ref, o_ref, fwd_sc):
    c = RET_CHUNK
    n_chunks = q_ref.shape[0] // c
    nt_dims = (((1,), (1,)), ((), ()))
    lg_f = _log_sigmoid(decf_ref[0])
    lg_b = _log_sigmoid(decb_ref[0])
    lg_f_k, lg_b_k = lg_f[:, :1], lg_b[:, :1]

    row = lax.broadcasted_iota(jnp.int32, (c, c), 0)
    col = lax.broadcasted_iota(jnp.int32, (c, c), 1)
    rel = (row - col).astype(F32)
    fwd_mask = rel >= 0
    bwd_mask = rel < 0
    d_f = jnp.where(fwd_mask, jnp.exp(jnp.where(fwd_mask, rel, 0.0) * lg_f_k), 0.0)
    d_b = jnp.where(bwd_mask, jnp.exp(jnp.where(bwd_mask, -rel, 0.0) * lg_b_k), 0.0)
    r_v = lax.broadcasted_iota(jnp.int32, (c, R_V_DIM), 0).astype(F32)
    r_k = lax.broadcasted_iota(jnp.int32, (c, R_QK_DIM), 0).astype(F32)
    qd_f = jnp.exp((r_v + 1.0) * lg_f)
    kd_f = jnp.exp((c - 1.0 - r_k) * lg_f_k)
    cd_f = jnp.exp(c * lg_f)
    qd_b = jnp.exp((c - r_v) * lg_b)
    kd_b = jnp.exp(r_k * lg_b_k)
    cd_b = jnp.exp(c * lg_b)

    def state_update(k_blk, v_blk, kd):
        kdec = (k_blk.astype(F32) * kd).T.astype(BF16)
        return jnp.dot(kdec, v_blk, preferred_element_type=F32)

    def chunk_out(q_blk, k_blk, v_blk, state, d_intra, qd):
        s = lax.dot_general(q_blk, k_blk, nt_dims, preferred_element_type=F32) * d_intra
        intra = jnp.dot(s.astype(BF16), v_blk, preferred_element_type=F32)
        inter = jnp.dot(q_blk, state.astype(BF16), preferred_element_type=F32) * qd
        return intra + inter

    r_m = lax.broadcasted_iota(jnp.int32, (CHUNK, R_QK_DIM), 0).astype(F32)
    state0 = state_update(mk_ref[...], mv_ref[...], jnp.exp((CHUNK - 1.0 - r_m) * lg_f_k))

    def fwd_body(i, state):
        off = pl.multiple_of(i * c, c)
        q_blk, k_blk, v_blk = q_ref[pl.ds(off, c), :], k_ref[pl.ds(off, c), :], v_ref[pl.ds(off, c), :]
        fwd_sc[pl.ds(off, c), :] = chunk_out(q_blk, k_blk, v_blk, state, d_f, qd_f)
        return state * cd_f + state_update(k_blk, v_blk, kd_f)

    lax.fori_loop(0, n_chunks, fwd_body, state0, unroll=RET_UNROLL)

    gain = gain_ref[0]

    def bwd_body(i, state):
        off = pl.multiple_of((n_chunks - 1 - i) * c, c)
        q_blk, k_blk, v_blk = q_ref[pl.ds(off, c), :], k_ref[pl.ds(off, c), :], v_ref[pl.ds(off, c), :]
        ro = fwd_sc[pl.ds(off, c), :] + chunk_out(q_blk, k_blk, v_blk, state, d_b, qd_b)
        ms = jnp.mean(ro * ro, axis=-1, keepdims=True)
        y = (ro * lax.rsqrt(ms + NORM_EPS)) * gain
        o_ref[pl.ds(off, c), :] = (y * _silu(g_ref[pl.ds(off, c), :].astype(F32))).astype(o_ref.dtype)
        return state * cd_b + state_update(k_blk, v_blk, kd_b)

    lax.fori_loop(0, n_chunks, bwd_body, jnp.zeros((R_QK_DIM, R_V_DIM), F32), unroll=RET_UNROLL)


def _retention(projs, metas, dec_f, dec_b, ret_gain, batch, seq):
    t = batch * seq
    kb, vb = R_QK_DIM, R_V_DIM
    _, proj_p, proj_r = projs
    _, meta_p, meta_r = metas
    q_spec = pl.BlockSpec((seq, kb), lambda b, h: (b, h))
    k_spec = pl.BlockSpec((seq, kb), lambda b, h: (b, R_HEADS + h))
    v_spec = pl.BlockSpec((seq, vb), lambda b, h: (b, 2 * A_HEADS + h))
    g_spec = pl.BlockSpec((seq, vb), lambda b, h: (b, 2 * A_HEADS + R_HEADS + h))
    mk_spec = pl.BlockSpec((CHUNK, kb), lambda b, h: (0, R_HEADS + h))
    mv_spec = pl.BlockSpec((CHUNK, vb), lambda b, h: (0, 2 * A_HEADS + h))
    head_spec = pl.BlockSpec((1, 1, vb), lambda b, h: (h, 0, 0))
    vmem = 2 * seq * (2 * kb + 3 * vb) * 2 + seq * vb * 4 + 64 * RET_CHUNK * vb * 4
    return pl.pallas_call(
        _ret_kernel,
        grid=(batch, R_HEADS),
        in_specs=[q_spec, k_spec, v_spec, g_spec, mk_spec, mv_spec, head_spec, head_spec, head_spec],
        out_specs=pl.BlockSpec((seq, vb), lambda b, h: (b, h)),
        out_shape=jax.ShapeDtypeStruct((t, R_HEADS * R_V_DIM), BF16),
        scratch_shapes=[pltpu.VMEM((seq, vb), F32)],
        compiler_params=_params(("arbitrary", "arbitrary"), vmem),
        name="retention",
    )(proj_r, proj_r, proj_p, proj_p, meta_r, meta_p, dec_f, dec_b, ret_gain)


def _out_kernel(a_ref, r_ref, w_ref, x_ref, g_ref, o_ref):
    half = a_ref.shape[1]
    h = x_ref[...] + jnp.dot(a_ref[...], w_ref[pl.ds(0, half), :], preferred_element_type=F32)
    h = h + jnp.dot(r_ref[...], w_ref[pl.ds(half, half), :], preferred_element_type=F32)
    ms = jnp.mean(h * h, axis=-1, keepdims=True)
    o_ref[...] = (h * lax.rsqrt(ms + NORM_EPS)) * g_ref[...]


def _out_proj(ao, ro, w_bf16, x2d, gain, tm):
    t, d = x2d.shape
    tm = min(tm, t)
    half = d // 2
    vmem = d * d * 2 + 2 * 2 * tm * half * 2 + 2 * 2 * tm * d * 4 + 2 * tm * d * 4
    return pl.pallas_call(
        _out_kernel,
        grid=(t // tm,),
        in_specs=[
            pl.BlockSpec((tm, half), lambda i: (i, 0)),
            pl.BlockSpec((tm, half), lambda i: (i, 0)),
            pl.BlockSpec((d, d), lambda i: (0, 0), pipeline_mode=pl.Buffered(1)),
            pl.BlockSpec((tm, d), lambda i: (i, 0)),
            pl.BlockSpec((1, d), lambda i: (0, 0)),
        ],
        out_specs=pl.BlockSpec((tm, d), lambda i: (i, 0)),
        out_shape=jax.ShapeDtypeStruct((t, d), F32),
        compiler_params=_params(("arbitrary",), vmem),
        name="out_proj",
    )(ao, ro, w_bf16, x2d, gain.reshape(1, d))


def _trunk(x, shared):
    batch, seq, d = x.shape
    x2d = x.reshape(batch * seq, d)
    u = _rmsnorm(x2d, shared["norm_gain"], NORM_TM)
    projs = _project(u, shared["w_in"], _rotary_tables(jnp.arange(seq, dtype=jnp.int32) + N_META), seq)
    ao = _attention(projs, shared["proj_meta"], shared["mbias"], shared["lams"], shared["subln_gain"],
                    batch, seq, ATTN_TQ, ATTN_TK)
    ro = _retention(projs, shared["proj_meta"], shared["dec_f"], shared["dec_b"], shared["ret_gain"], batch, seq)
    y = _out_proj(ao, ro, shared["w_out"], x2d, shared["final_gain"], OUT_TM)
    return y.reshape(batch, seq, d)


def kernel(x_prompt, x_sample, meta_tokens, norm_gain, w_in, lambda_q1, lambda_k1, lambda_q2, lambda_k2,
           attn_subln_gain, ret_decay_fwd, ret_decay_bwd, ret_norm_gain, w_out, final_norm_gain):
    assert norm_gain.shape[0] == 1, "single-layer trunk"
    d = x_prompt.shape[-1]
    w_in_bf16 = w_in[0].astype(BF16)
    w_out_bf16 = w_out[0].astype(BF16)

    meta_pos = jnp.arange(CHUNK, dtype=jnp.int32) - (CHUNK - N_META)
    meta_rows = jnp.concatenate([jnp.zeros((CHUNK - N_META, d), F32), meta_tokens.astype(F32)], axis=0)
    u_meta = _rmsnorm(meta_rows, norm_gain[0], CHUNK)
    proj_meta = _project(u_meta, w_in_bf16, _rotary_tables(meta_pos), CHUNK)
    mbias = jnp.where(meta_pos >= 0, 0.0, NEG_BIG).astype(F32).reshape(1, CHUNK)

    def per_head(v):
        return jnp.broadcast_to(v.astype(F32)[:, None, None], (R_HEADS, 1, R_V_DIM))

    shared = dict(
        norm_gain=norm_gain[0], w_in=w_in_bf16, w_out=w_out_bf16, proj_meta=proj_meta, mbias=mbias,
        lams=tuple(v[0].astype(F32).reshape(1, A_QK_DIM) for v in (lambda_q1, lambda_k1, lambda_q2, lambda_k2)),
        subln_gain=attn_subln_gain[0].astype(F32),
        dec_f=per_head(ret_decay_fwd[0]), dec_b=per_head(ret_decay_bwd[0]),
        ret_gain=ret_norm_gain[0].astype(F32).reshape(R_HEADS, 1, R_V_DIM),
        final_gain=final_norm_gain.astype(F32),
    )
    return (_trunk(x_prompt, shared), _trunk(x_sample, shared))
```

```python
import functools
import math

import jax
import jax.numpy as jnp
import numpy as np
from jax import lax
from jax.experimental import pallas as pl
from jax.experimental.pallas import tpu as pltpu

F32 = jnp.float32
BF16 = jnp.bfloat16

D_MODEL = 4096
N_META = 16
CHUNK = 128
A_HEADS = 8
A_QK_DIM = 128
A_V_DIM = 256
A_ROT_DIM = A_QK_DIM // 4
ROPE_THETA = 500000.0
R_HEADS = 8
R_QK_DIM = 128
R_V_DIM = 256
RET_THETA = 10000.0
NORM_EPS = 1e-6
NEG_BIG = -1e30
LAMBDA_INIT = 0.8 - 0.6 * math.exp(-0.3 * 0)

AQ_OFF, AK_OFF, AV_OFF, AG_OFF = 0, 2048, 4096, 6144
RQ_OFF, RK_OFF, RV_OFF, RG_OFF = 8192, 9216, 10240, 12288
PROJ_OUT = 14336

LANES = 128
LOG2E = math.log2(math.e)
ATTN_SCORE_SCALE = (A_QK_DIM ** -0.5) * LOG2E
V7X_VMEM_BYTES = 64 * 1024 * 1024
V7X_VMEM_BUDGET = 60 * 1024 * 1024

PROJ_TM = 1024
PROJ_TN = 1024
PROJ_RM = 256
ATTN_TQ = 1024
ATTN_TK = 512
ATTN_RT = 128
ATTN_UNROLL = 5
OUT_TM = 256
RET_CHUNK = 256
RET_UNROLL = 4
NORM_TM = 512


def _params(sem, vmem_bytes):
    return pltpu.CompilerParams(
        dimension_semantics=sem, vmem_limit_bytes=min(int(vmem_bytes), V7X_VMEM_BUDGET)
    )


def _rmsnorm_kernel(x_ref, g_ref, o_ref):
    x = x_ref[...]
    ms = jnp.mean(x * x, axis=-1, keepdims=True)
    y = x * lax.rsqrt(ms + NORM_EPS)
    o_ref[...] = (y * g_ref[...]).astype(o_ref.dtype)


def _rmsnorm(x2d, gain, tm):
    t, d = x2d.shape
    tm = min(tm, t)
    return pl.pallas_call(
        _rmsnorm_kernel,
        grid=(t // tm,),
        in_specs=[pl.BlockSpec((tm, d), lambda i: (i, 0)), pl.BlockSpec((1, d), lambda i: (0, 0))],
        out_specs=pl.BlockSpec((tm, d), lambda i: (i, 0)),
        out_shape=jax.ShapeDtypeStruct((t, d), BF16),
        compiler_params=_params(("arbitrary",), 2 * tm * d * (4 + 2) + 4 * tm * d * 4),
        name="rmsnorm_in",
    )(x2d, gain.reshape(1, d))


def _proj_plain_kernel(u_ref, w_ref, o_ref, *, rm):
    for c in range(u_ref.shape[0] // rm):
        rows = pl.ds(c * rm, rm)
        o_ref[rows, :] = jnp.dot(u_ref[rows, :], w_ref[...], preferred_element_type=F32).astype(o_ref.dtype)


def _proj_rot_kernel(u_ref, w_ref, cos_ref, sin1_ref, sin2_ref, mult_ref, o_ref, *, rm, shift1, shift2):
    for c in range(u_ref.shape[0] // rm):
        rows = pl.ds(c * rm, rm)
        acc = jnp.dot(u_ref[rows, :], w_ref[...], preferred_element_type=F32)
        cos, sin1, sin2, mult = cos_ref[rows, :], sin1_ref[rows, :], sin2_ref[rows, :], mult_ref[rows, :]
        for g in range(acc.shape[1] // LANES):
            x = acc[:, g * LANES:(g + 1) * LANES]
            y = x * cos + pltpu.roll(x, shift1, 1) * sin1 + pltpu.roll(x, shift2, 1) * sin2
            o_ref[rows, g * LANES:(g + 1) * LANES] = (y * mult).astype(o_ref.dtype)


def _proj_plain(u, w_bf16, tm, tn, rm):
    t, d = u.shape
    tm = min(tm, t)
    n_a = (RQ_OFF - AV_OFF) // tn
    n_cols = (RQ_OFF - AV_OFF) + (PROJ_OUT - RV_OFF)
    vmem = 2 * tm * d * 2 + 2 * d * tn * 2 + 2 * tm * tn * 2 + 2 * min(rm, tm) * tn * 4
    return pl.pallas_call(
        functools.partial(_proj_plain_kernel, rm=min(rm, tm)),
        grid=(t // tm, n_cols // tn),
        in_specs=[pl.BlockSpec((tm, d), lambda i, j: (i, 0)),
                  pl.BlockSpec((d, tn), lambda i, j: (0, jnp.where(j < n_a, AV_OFF // tn + j, RV_OFF // tn + j - n_a)))],
        out_specs=pl.BlockSpec((tm, tn), lambda i, j: (i, j)),
        out_shape=jax.ShapeDtypeStruct((t, n_cols), BF16),
        compiler_params=_params(("arbitrary", "arbitrary"), vmem),
        name="in_proj_plain",
    )(u, w_bf16)


def _proj_rot(u, w_bf16, tables, mults, col_off, n_cols, shifts, rows_per_seq, tm, tn, rm, name):
    t, d = u.shape
    tm = min(tm, rows_per_seq)
    n_seq_tiles = rows_per_seq // tm
    n_j = n_cols // tn
    tab_spec = pl.BlockSpec((tm, LANES), lambda i, j: (i % n_seq_tiles, 0))
    mult_spec = pl.BlockSpec((None, tm, LANES), lambda i, j: (j // (n_j // 2), i % n_seq_tiles, 0))
    vmem = 2 * tm * d * 2 + 2 * d * tn * 2 + 2 * tm * tn * 2 + 2 * 4 * tm * LANES * 4 + 3 * min(rm, tm) * tn * 4
    return pl.pallas_call(
        functools.partial(_proj_rot_kernel, rm=min(rm, tm), shift1=shifts[0], shift2=shifts[1]),
        grid=(t // tm, n_j),
        in_specs=[pl.BlockSpec((tm, d), lambda i, j: (i, 0)),
                  pl.BlockSpec((d, tn), lambda i, j: (0, col_off // tn + j)),
                  tab_spec, tab_spec, tab_spec, mult_spec],
        out_specs=pl.BlockSpec((tm, tn), lambda i, j: (i, j)),
        out_shape=jax.ShapeDtypeStruct((t, n_cols), BF16),
        compiler_params=_params(("arbitrary", "arbitrary"), vmem),
        name=name,
    )(u, w_bf16, *tables, mults)


def _project(u, w_bf16, tabs, rows_per_seq):
    proj_a = _proj_rot(u, w_bf16, tabs["attn"], tabs["attn_mult"], AQ_OFF, AV_OFF - AQ_OFF,
                       (LANES - A_ROT_DIM // 2, A_ROT_DIM // 2), rows_per_seq, PROJ_TM, PROJ_TN, PROJ_RM, "in_proj_attn")
    proj_p = _proj_plain(u, w_bf16, PROJ_TM, PROJ_TN, PROJ_RM)
    proj_r = _proj_rot(u, w_bf16, tabs["ret"], tabs["ret_mult"], RQ_OFF, RV_OFF - RQ_OFF,
                       (R_QK_DIM // 2, R_QK_DIM // 2), rows_per_seq, PROJ_TM, PROJ_TN, PROJ_RM, "in_proj_ret")
    return proj_a, proj_p, proj_r


def _rotary_tables(pos):
    n = pos.shape[0]
    posf = pos.astype(F32)[:, None]
    half = A_ROT_DIM // 2
    inv = jnp.power(jnp.float32(ROPE_THETA), -jnp.arange(half, dtype=F32) * 2.0 / A_ROT_DIM)
    ang = posf * inv[None, :]
    cos, sin = jnp.cos(ang), jnp.sin(ang)
    ones = jnp.ones((n, LANES - A_ROT_DIM), F32)
    zeros_h = jnp.zeros((n, half), F32)
    zeros_r = jnp.zeros((n, LANES - A_ROT_DIM), F32)
    cos_a = jnp.concatenate([cos, cos, ones], axis=1)
    sin_hi = jnp.concatenate([-sin, zeros_h, zeros_r], axis=1)
    sin_lo = jnp.concatenate([zeros_h, sin, zeros_r], axis=1)
    half_r = R_QK_DIM // 2
    inv_r = jnp.power(jnp.float32(RET_THETA), -jnp.arange(half_r, dtype=F32) * 2.0 / R_QK_DIM)
    ang_r = posf * inv_r[None, :]
    cos_r = jnp.concatenate([jnp.cos(ang_r), jnp.cos(ang_r)], axis=1)
    sin_r = jnp.concatenate([-jnp.sin(ang_r), jnp.sin(ang_r)], axis=1)
    one = jnp.ones((n, LANES), F32)
    kmask = jnp.broadcast_to((pos >= 0).astype(F32)[:, None], (n, LANES))
    return dict(
        attn=(cos_a, sin_hi, sin_lo),
        attn_mult=jnp.stack([one * ATTN_SCORE_SCALE, one]),
        ret=(cos_r, sin_r, jnp.zeros((n, LANES), F32)),
        ret_mult=jnp.stack([one, (R_QK_DIM ** -0.5) * kmask]),
    )


def _silu(g):
    return g * (1.0 / (1.0 + jnp.exp(-g)))


def _attn_kernel(q_ref, k_ref, v_ref, g_ref, mk_ref, mv_ref, mbias_ref, lq1_ref, lk1_ref, lq2_ref, lk2_ref,
                 gain_ref, o_ref, m_sc, l_sc, acc_sc, *, tk, rt):
    tq = q_ref.shape[0]
    n_kb = k_ref.shape[0] // tk
    nt_dims = (((1,), (1,)), ((), ()))
    sub = (pl.ds(0, A_QK_DIM), pl.ds(A_QK_DIM, A_QK_DIM))

    def lane_tiles(x):
        return [x[:, t * LANES:(t + 1) * LANES] for t in range(x.shape[1] // LANES)]

    def update(idx, rows, s, v_blk, first):
        tiles = lane_tiles(s)
        mx = functools.reduce(jnp.maximum, tiles)
        m_cur = jnp.max(mx, axis=-1, keepdims=True)
        if first:
            m_new = jnp.broadcast_to(m_cur, (rt, LANES))
        else:
            m_prev = m_sc[idx, rows, :]
            m_new = jnp.maximum(m_prev, m_cur)
        p_tiles = [jnp.exp2(t - m_new) for t in tiles]
        p = jnp.concatenate(p_tiles, axis=1) if len(p_tiles) > 1 else p_tiles[0]
        pv = jnp.dot(p.astype(BF16), v_blk, preferred_element_type=F32)
        l_part = functools.reduce(jnp.add, p_tiles)
        if first:
            l_sc[idx, rows, :] = l_part
            acc_sc[idx, rows, :] = pv
        else:
            alpha = jnp.exp2(m_prev - m_new)
            l_sc[idx, rows, :] = alpha * l_sc[idx, rows, :] + l_part
            acc_sc[idx, rows, :] = jnp.concatenate([alpha, alpha], axis=1) * acc_sc[idx, rows, :] + pv
        m_sc[idx, rows, :] = m_new

    def scores(rows, idx, keys):
        return lax.dot_general(q_ref[rows, sub[idx]], keys, nt_dims, preferred_element_type=F32)

    mbias = mbias_ref[...] * LOG2E
    v_first = jnp.concatenate([mv_ref[...], v_ref[pl.ds(0, tk), :]], axis=0)
    for r in range(tq // rt):
        rows = pl.ds(r * rt, rt)
        for idx in range(2):
            s_meta = scores(rows, idx, mk_ref[:, sub[idx]]) + mbias
            s_tok = scores(rows, idx, k_ref[pl.ds(0, tk), sub[idx]])
            update(idx, rows, jnp.concatenate([s_meta, s_tok], axis=1), v_first, True)

    def body(kb, carry):
        off = pl.multiple_of(kb * tk, tk)
        v_blk = v_ref[pl.ds(off, tk), :]
        for r in range(tq // rt):
            rows = pl.ds(r * rt, rt)
            for idx in range(2):
                update(idx, rows, scores(rows, idx, k_ref[pl.ds(off, tk), sub[idx]]), v_blk, False)
        return carry

    lax.fori_loop(1, n_kb, body, 0, unroll=ATTN_UNROLL)

    lam = (jnp.exp(jnp.sum(lq1_ref[...] * lk1_ref[...], axis=-1, keepdims=True))
           - jnp.exp(jnp.sum(lq2_ref[...] * lk2_ref[...], axis=-1, keepdims=True)) + LAMBDA_INIT)
    inv1 = 1.0 / jnp.sum(l_sc[0], axis=-1, keepdims=True)
    inv2 = lam / jnp.sum(l_sc[1], axis=-1, keepdims=True)
    o = acc_sc[0] * inv1 - acc_sc[1] * inv2
    ms = jnp.mean(o * o, axis=-1, keepdims=True)
    y = (o * lax.rsqrt(ms + NORM_EPS)) * (gain_ref[...] * (1.0 - LAMBDA_INIT))
    o_ref[...] = (y * _silu(g_ref[...].astype(F32))).astype(o_ref.dtype)


def _attention(projs, metas, mbias, lams, subln_gain, batch, seq, tq, tk):
    t = batch * seq
    tq, tk = min(tq, seq), min(tk, seq)
    nq = seq // tq
    hb = A_V_DIM
    proj_a, proj_p, _ = projs
    meta_a, meta_p, _ = metas
    q_spec = pl.BlockSpec((tq, hb), lambda b, h, i: (b * nq + i, h))
    k_spec = pl.BlockSpec((seq, hb), lambda b, h, i: (b, A_HEADS + h))
    v_spec = pl.BlockSpec((seq, hb), lambda b, h, i: (b, h))
    g_spec = pl.BlockSpec((tq, hb), lambda b, h, i: (b * nq + i, A_HEADS + h))
    mk_spec = pl.BlockSpec((CHUNK, hb), lambda b, h, i: (0, A_HEADS + h))
    mv_spec = pl.BlockSpec((CHUNK, hb), lambda b, h, i: (0, h))
    row128 = pl.BlockSpec((1, LANES), lambda b, h, i: (0, 0))
    gain_spec = pl.BlockSpec((1, A_V_DIM), lambda b, h, i: (0, 0))
    vmem = (2 * 2 * seq * hb * 2 + 2 * 3 * tq * hb * 2 + 2 * 2 * CHUNK * hb * 2
            + 2 * tq * (2 * LANES + A_V_DIM) * 4 + 6 * tq * tk * 4)
    return pl.pallas_call(
        functools.partial(_attn_kernel, tk=tk, rt=min(ATTN_RT, tq)),
        grid=(batch, A_HEADS, nq),
        in_specs=[q_spec, k_spec, v_spec, g_spec, mk_spec, mv_spec, row128, row128, row128, row128, row128,
                  gain_spec],
        out_specs=pl.BlockSpec((tq, hb), lambda b, h, i: (b * nq + i, h)),
        out_shape=jax.ShapeDtypeStruct((t, A_HEADS * A_V_DIM), BF16),
        scratch_shapes=[pltpu.VMEM((2, tq, LANES), F32), pltpu.VMEM((2, tq, LANES), F32),
                        pltpu.VMEM((2, tq, A_V_DIM), F32)],
        compiler_params=_params(("arbitrary", "arbitrary", "arbitrary"), vmem),
        name="diff_attention",
    )(proj_a, proj_a, proj_p, proj_p, meta_a, meta_p, mbias, *lams, subln_gain.reshape(1, A_V_DIM))


def _log_sigmoid(x):
    return jnp.minimum(x, 0.0) - jnp.log(1.0 + jnp.exp(-jnp.abs(x)))


def _ret_kernel(q_ref, k_ref, v_ref, g_ref, mk_ref, mv_ref, decf_ref, decb_ref, gain_ref, o_ref, u_sc, st_sc):
    c = RET_CHUNK
    dk, dv = R_QK_DIM, R_V_DIM
    n_chunks = q_ref.shape[0] // c
    nt_dims = (((1,), (1,)), ((), ()))
    lg_f = _log_sigmoid(decf_ref[0])
    lg_b = _log_sigmoid(decb_ref[0])
    lg_f1, lg_b1 = lg_f[:, :1], lg_b[:, :1]

    row = lax.broadcasted_iota(jnp.int32, (c, c), 0)
    col = lax.broadcasted_iota(jnp.int32, (c, c), 1)
    rel = (row - col).astype(F32)
    fwd_mask = rel >= 0
    d_both = jnp.where(fwd_mask, jnp.exp(jnp.where(fwd_mask, rel, 0.0) * lg_f1),
                       jnp.exp(jnp.where(fwd_mask, 0.0, -rel) * lg_b1))
    r_v = lax.broadcasted_iota(jnp.int32, (c, dv), 0).astype(F32)
    r_k = lax.broadcasted_iota(jnp.int32, (c, dk), 0).astype(F32)
    qd = jnp.concatenate([jnp.exp((r_v + 1.0) * lg_f), jnp.exp((c - r_v) * lg_b)], axis=1)
    kd = jnp.concatenate([jnp.exp((c - 1.0 - r_k) * lg_f1), jnp.exp(r_k * lg_b1)], axis=1)
    cd_f = jnp.exp(c * lg_f)
    cd_b = jnp.exp(c * lg_b)

    def chunk_rows(i):
        return pl.ds(pl.multiple_of(i * c, c), c)

    def loop_a(i, carry):
        rows = chunk_rows(i)
        k_blk = k_ref[rows, :].astype(F32)
        kdec = (jnp.concatenate([k_blk, k_blk], axis=1) * kd).T.astype(BF16)
        u_sc[i] = jnp.dot(kdec, v_ref[rows, :], preferred_element_type=F32)
        return carry

    lax.fori_loop(0, n_chunks, loop_a, 0, unroll=RET_UNROLL)

    r_m = lax.broadcasted_iota(jnp.int32, (CHUNK, dk), 0).astype(F32)
    mk_dec = (mk_ref[...].astype(F32) * jnp.exp((CHUNK - 1.0 - r_m) * lg_f1)).T.astype(BF16)
    state0 = jnp.dot(mk_dec, mv_ref[...], preferred_element_type=F32)

    def loop_b(i, carry):
        st_f, st_b = carry
        j = n_chunks - 1 - i
        st_sc[i, :, pl.ds(0, dv)] = st_f.astype(BF16)
        st_sc[j, :, pl.ds(dv, dv)] = st_b.astype(BF16)
        return st_f * cd_f + u_sc[i, pl.ds(0, dk), :], st_b * cd_b + u_sc[j, pl.ds(dk, dk), :]

    lax.fori_loop(0, n_chunks, loop_b, (state0, jnp.zeros((dk, dv), F32)), unroll=RET_UNROLL)

    gain = gain_ref[0]

    def loop_c(i, carry):
        rows = chunk_rows(i)
        q_blk, v_blk = q_ref[rows, :], v_ref[rows, :]
        s = lax.dot_general(q_blk, k_ref[rows, :], nt_dims, preferred_element_type=F32) * d_both
        intra = jnp.dot(s.astype(BF16), v_blk, preferred_element_type=F32)
        inter = jnp.dot(q_blk, st_sc[i], preferred_element_type=F32) * qd
        ro = intra + (inter[:, :dv] + inter[:, dv:])
        ms = jnp.mean(ro * ro, axis=-1, keepdims=True)
        y = (ro * lax.rsqrt(ms + NORM_EPS)) * gain
        o_ref[rows, :] = (y * _silu(g_ref[rows, :].astype(F32))).astype(o_ref.dtype)
        return carry

    lax.fori_loop(0, n_chunks, loop_c, 0, unroll=RET_UNROLL)


def _retention(projs, metas, dec_f, dec_b, ret_gain, batch, seq):
    t = batch * seq
    kb, vb = R_QK_DIM, R_V_DIM
    _, proj_p, proj_r = projs
    _, meta_p, meta_r = metas
    q_spec = pl.BlockSpec((seq, kb), lambda b, h: (b, h))
    k_spec = pl.BlockSpec((seq, kb), lambda b, h: (b, R_HEADS + h))
    v_spec = pl.BlockSpec((seq, vb), lambda b, h: (b, 2 * A_HEADS + h))
    g_spec = pl.BlockSpec((seq, vb), lambda b, h: (b, 2 * A_HEADS + R_HEADS + h))
    mk_spec = pl.BlockSpec((CHUNK, kb), lambda b, h: (0, R_HEADS + h))
    mv_spec = pl.BlockSpec((CHUNK, vb), lambda b, h: (0, 2 * A_HEADS + h))
    head_spec = pl.BlockSpec((1, 1, vb), lambda b, h: (h, 0, 0))
    n_chunks = seq // RET_CHUNK
    vmem = 2 * seq * (2 * kb + 3 * vb) * 2 + n_chunks * 2 * kb * vb * (4 + 2) + 64 * RET_CHUNK * vb * 4
    return pl.pallas_call(
        _ret_kernel,
        grid=(batch, R_HEADS),
        in_specs=[q_spec, k_spec, v_spec, g_spec, mk_spec, mv_spec, head_spec, head_spec, head_spec],
        out_specs=pl.BlockSpec((seq, vb), lambda b, h: (b, h)),
        out_shape=jax.ShapeDtypeStruct((t, R_HEADS * R_V_DIM), BF16),
        scratch_shapes=[pltpu.VMEM((n_chunks, 2 * kb, vb), F32), pltpu.VMEM((n_chunks, kb, 2 * vb), BF16)],
        compiler_params=_params(("arbitrary", "arbitrary"), vmem),
        name="retention",
    )(proj_r, proj_r, proj_p, proj_p, meta_r, meta_p, dec_f, dec_b, ret_gain)


def _out_kernel(a_ref, r_ref, w_ref, x_ref, g_ref, o_ref):
    half = a_ref.shape[1]
    h = x_ref[...] + jnp.dot(a_ref[...], w_ref[pl.ds(0, half), :], preferred_element_type=F32)
    h = h + jnp.dot(r_ref[...], w_ref[pl.ds(half, half), :], preferred_element_type=F32)
    ms = jnp.mean(h * h, axis=-1, keepdims=True)
    o_ref[...] = (h * lax.rsqrt(ms + NORM_EPS)) * g_ref[...]


def _out_proj(ao, ro, w_bf16, x2d, gain, tm):
    t, d = x2d.shape
    tm = min(tm, t)
    half = d // 2
    vmem = d * d * 2 + 2 * 2 * tm * half * 2 + 2 * 2 * tm * d * 4 + 2 * tm * d * 4
    return pl.pallas_call(
        _out_kernel,
        grid=(t // tm,),
        in_specs=[
            pl.BlockSpec((tm, half), lambda i: (i, 0)),
            pl.BlockSpec((tm, half), lambda i: (i, 0)),
            pl.BlockSpec((d, d), lambda i: (0, 0), pipeline_mode=pl.Buffered(1)),
            pl.BlockSpec((tm, d), lambda i: (i, 0)),
            pl.BlockSpec((1, d), lambda i: (0, 0)),
        ],
        out_specs=pl.BlockSpec((tm, d), lambda i: (i, 0)),
        out_shape=jax.ShapeDtypeStruct((t, d), F32),
        compiler_params=_params(("arbitrary",), vmem),
        name="out_proj",
    )(ao, ro, w_bf16, x2d, gain.reshape(1, d))


def _trunk(x, shared):
    batch, seq, d = x.shape
    x2d = x.reshape(batch * seq, d)
    u = _rmsnorm(x2d, shared["norm_gain"], NORM_TM)
    projs = _project(u, shared["w_in"], _rotary_tables(jnp.arange(seq, dtype=jnp.int32) + N_META), seq)
    ao = _attention(projs, shared["proj_meta"], shared["mbias"], shared["lams"], shared["subln_gain"],
                    batch, seq, ATTN_TQ, ATTN_TK)
    ro = _retention(projs, shared["proj_meta"], shared["dec_f"], shared["dec_b"], shared["ret_gain"], batch, seq)
    y = _out_proj(ao, ro, shared["w_out"], x2d, shared["final_gain"], OUT_TM)
    return y.reshape(batch, seq, d)


def kernel(x_prompt, x_sample, meta_tokens, norm_gain, w_in, lambda_q1, lambda_k1, lambda_q2, lambda_k2,
           attn_subln_gain, ret_decay_fwd, ret_decay_bwd, ret_norm_gain, w_out, final_norm_gain):
    assert norm_gain.shape[0] == 1, "single-layer trunk"
    d = x_prompt.shape[-1]
    w_in_bf16 = w_in[0].astype(BF16)
    w_out_bf16 = w_out[0].astype(BF16)

    meta_pos = jnp.arange(CHUNK, dtype=jnp.int32) - (CHUNK - N_META)
    meta_rows = jnp.concatenate([jnp.zeros((CHUNK - N_META, d), F32), meta_tokens.astype(F32)], axis=0)
    u_meta = _rmsnorm(meta_rows, norm_gain[0], CHUNK)
    proj_meta = _project(u_meta, w_in_bf16, _rotary_tables(meta_pos), CHUNK)
    mbias = jnp.where(meta_pos >= 0, 0.0, NEG_BIG).astype(F32).reshape(1, CHUNK)

    def per_head(v):
        return jnp.broadcast_to(v.astype(F32)[:, None, None], (R_HEADS, 1, R_V_DIM))

    shared = dict(
        norm_gain=norm_gain[0], w_in=w_in_bf16, w_out=w_out_bf16, proj_meta=proj_meta, mbias=mbias,
        lams=tuple(v[0].astype(F32).reshape(1, A_QK_DIM) for v in (lambda_q1, lambda_k1, lambda_q2, lambda_k2)),
        subln_gain=attn_subln_gain[0].astype(F32),
        dec_f=per_head(ret_decay_fwd[0]), dec_b=per_head(ret_decay_bwd[0]),
        ret_gain=ret_norm_gain[0].astype(F32).reshape(R_HEADS, 1, R_V_DIM),
        final_gain=final_norm_gain.astype(F32),
    )
    return (_trunk(x_prompt, shared), _trunk(x_sample, shared))
```

```python
import functools
import math

import jax
import jax.numpy as jnp
from jax import lax
from jax.experimental import pallas as pl
from jax.experimental.pallas import tpu as pltpu

F32 = jnp.float32
BF16 = jnp.bfloat16

D_MODEL = 4096
N_META = 16
CHUNK = 128
A_HEADS = 8
A_QK_DIM = 128
A_V_DIM = 256
A_ROT_DIM = A_QK_DIM // 4
ROPE_THETA = 500000.0
R_HEADS = 8
R_QK_DIM = 128
R_V_DIM = 256
RET_THETA = 10000.0
NORM_EPS = 1e-6
NEG_BIG = -1e30
LAMBDA_INIT = 0.8 - 0.6 * math.exp(-0.3 * 0)

AQ_OFF = 0
AV_OFF = AQ_OFF + 2 * (2 * A_HEADS * A_QK_DIM)
RQ_OFF = AV_OFF + 2 * (A_HEADS * A_V_DIM)
RV_OFF = RQ_OFF + 2 * (R_HEADS * R_QK_DIM)
PROJ_OUT = RV_OFF + 2 * (R_HEADS * R_V_DIM)

LANES = 128
LOG2E = math.log2(math.e)
ATTN_SCORE_SCALE = (A_QK_DIM ** -0.5) * LOG2E
V7X_VMEM_BYTES = 64 * 1024 * 1024
V7X_VMEM_BUDGET = V7X_VMEM_BYTES - 4 * 1024 * 1024

PROJ_TM = 1024
PROJ_TN = 1024
PROJ_RM = 256
ATTN_TQ = 1024
ATTN_TK = 512
ATTN_RT = 128
ATTN_GUARD_LOG2 = 64.0
ATTN_UNROLL = 5
OUT_TM = 256
RET_CHUNK = 256
RET_UNROLL = 4
NORM_TM = 512


def _params(sem, vmem_bytes):
    return pltpu.CompilerParams(
        dimension_semantics=sem, vmem_limit_bytes=min(int(vmem_bytes), V7X_VMEM_BUDGET)
    )


def _rmsnorm_kernel(x_ref, g_ref, o_ref):
    x = x_ref[...]
    ms = jnp.mean(x * x, axis=-1, keepdims=True)
    y = x * lax.rsqrt(ms + NORM_EPS)
    o_ref[...] = (y * g_ref[...]).astype(o_ref.dtype)


def _rmsnorm(x2d, gain, tm):
    t, d = x2d.shape
    tm = min(tm, t)
    return pl.pallas_call(
        _rmsnorm_kernel,
        grid=(t // tm,),
        in_specs=[pl.BlockSpec((tm, d), lambda i: (i, 0)), pl.BlockSpec((1, d), lambda i: (0, 0))],
        out_specs=pl.BlockSpec((tm, d), lambda i: (i, 0)),
        out_shape=jax.ShapeDtypeStruct((t, d), BF16),
        compiler_params=_params(("arbitrary",), 2 * tm * d * (4 + 2) + 4 * tm * d * 4),
        name="rmsnorm_in",
    )(x2d, gain.reshape(1, d))


def _proj_plain_kernel(u_ref, w_ref, o_ref, *, rm):
    for c in range(u_ref.shape[0] // rm):
        rows = pl.ds(c * rm, rm)
        o_ref[rows, :] = jnp.dot(u_ref[rows, :], w_ref[...], preferred_element_type=F32).astype(o_ref.dtype)


def _proj_rot_kernel(u_ref, w_ref, cos_ref, sin1_ref, sin2_ref, mult_ref, o_ref, *, rm, shift1, shift2):
    for c in range(u_ref.shape[0] // rm):
        rows = pl.ds(c * rm, rm)
        acc = jnp.dot(u_ref[rows, :], w_ref[...], preferred_element_type=F32)
        cos, sin1, sin2, mult = cos_ref[rows, :], sin1_ref[rows, :], sin2_ref[rows, :], mult_ref[rows, :]
        for g in range(acc.shape[1] // LANES):
            x = acc[:, g * LANES:(g + 1) * LANES]
            y = x * cos + pltpu.roll(x, shift1, 1) * sin1 + pltpu.roll(x, shift2, 1) * sin2
            o_ref[rows, g * LANES:(g + 1) * LANES] = (y * mult).astype(o_ref.dtype)


def _proj_plain(u, w_bf16, tm, tn, rm):
    t, d = u.shape
    tm = min(tm, t)
    n_a = (RQ_OFF - AV_OFF) // tn
    n_cols = (RQ_OFF - AV_OFF) + (PROJ_OUT - RV_OFF)
    vmem = 2 * tm * d * 2 + 2 * d * tn * 2 + 2 * tm * tn * 2 + 2 * min(rm, tm) * tn * 4
    return pl.pallas_call(
        functools.partial(_proj_plain_kernel, rm=min(rm, tm)),
        grid=(t // tm, n_cols // tn),
        in_specs=[pl.BlockSpec((tm, d), lambda i, j: (i, 0)),
                  pl.BlockSpec((d, tn), lambda i, j: (0, jnp.where(j < n_a, AV_OFF // tn + j, RV_OFF // tn + j - n_a)))],
        out_specs=pl.BlockSpec((tm, tn), lambda i, j: (i, j)),
        out_shape=jax.ShapeDtypeStruct((t, n_cols), BF16),
        compiler_params=_params(("arbitrary", "arbitrary"), vmem),
        name="in_proj_plain",
    )(u, w_bf16)


def _proj_rot(u, w_bf16, tables, mults, col_off, n_cols, shifts, rows_per_seq, tm, tn, rm, name):
    t, d = u.shape
    tm = min(tm, rows_per_seq)
    n_seq_tiles = rows_per_seq // tm
    n_j = n_cols // tn
    tab_spec = pl.BlockSpec((tm, LANES), lambda i, j: (i % n_seq_tiles, 0))
    mult_spec = pl.BlockSpec((None, tm, LANES), lambda i, j: (j // (n_j // 2), i % n_seq_tiles, 0))
    vmem = 2 * tm * d * 2 + 2 * d * tn * 2 + 2 * tm * tn * 2 + 2 * 4 * tm * LANES * 4 + 3 * min(rm, tm) * tn * 4
    return pl.pallas_call(
        functools.partial(_proj_rot_kernel, rm=min(rm, tm), shift1=shifts[0], shift2=shifts[1]),
        grid=(t // tm, n_j),
        in_specs=[pl.BlockSpec((tm, d), lambda i, j: (i, 0)),
                  pl.BlockSpec((d, tn), lambda i, j: (0, col_off // tn + j)),
                  tab_spec, tab_spec, tab_spec, mult_spec],
        out_specs=pl.BlockSpec((tm, tn), lambda i, j: (i, j)),
        out_shape=jax.ShapeDtypeStruct((t, n_cols), BF16),
        compiler_params=_params(("arbitrary", "arbitrary"), vmem),
        name=name,
    )(u, w_bf16, *tables, mults)


def _project(u, w_bf16, tabs, rows_per_seq):
    proj_a = _proj_rot(u, w_bf16, tabs["attn"], tabs["attn_mult"], AQ_OFF, AV_OFF - AQ_OFF,
                       (LANES - A_ROT_DIM // 2, A_ROT_DIM // 2), rows_per_seq, PROJ_TM, PROJ_TN, PROJ_RM, "in_proj_attn")
    proj_p = _proj_plain(u, w_bf16, PROJ_TM, PROJ_TN, PROJ_RM)
    proj_r = _proj_rot(u, w_bf16, tabs["ret"], tabs["ret_mult"], RQ_OFF, RV_OFF - RQ_OFF,
                       (R_QK_DIM // 2, R_QK_DIM // 2), rows_per_seq, PROJ_TM, PROJ_TN, PROJ_RM, "in_proj_ret")
    return proj_a, proj_p, proj_r


def _rotary_tables(pos):
    n = pos.shape[0]
    posf = pos.astype(F32)[:, None]
    half = A_ROT_DIM // 2
    inv = jnp.power(jnp.float32(ROPE_THETA), -jnp.arange(half, dtype=F32) * 2.0 / A_ROT_DIM)
    ang = posf * inv[None, :]
    cos, sin = jnp.cos(ang), jnp.sin(ang)
    ones = jnp.ones((n, LANES - A_ROT_DIM), F32)
    zeros_h = jnp.zeros((n, half), F32)
    zeros_r = jnp.zeros((n, LANES - A_ROT_DIM), F32)
    cos_a = jnp.concatenate([cos, cos, ones], axis=1)
    sin_hi = jnp.concatenate([-sin, zeros_h, zeros_r], axis=1)
    sin_lo = jnp.concatenate([zeros_h, sin, zeros_r], axis=1)
    half_r = R_QK_DIM // 2
    inv_r = jnp.power(jnp.float32(RET_THETA), -jnp.arange(half_r, dtype=F32) * 2.0 / R_QK_DIM)
    ang_r = posf * inv_r[None, :]
    cos_r = jnp.concatenate([jnp.cos(ang_r), jnp.cos(ang_r)], axis=1)
    sin_r = jnp.concatenate([-jnp.sin(ang_r), jnp.sin(ang_r)], axis=1)
    one = jnp.ones((n, LANES), F32)
    kmask = jnp.broadcast_to((pos >= 0).astype(F32)[:, None], (n, LANES))
    return dict(
        attn=(cos_a, sin_hi, sin_lo),
        attn_mult=jnp.stack([one * ATTN_SCORE_SCALE, one]),
        ret=(cos_r, sin_r, jnp.zeros((n, LANES), F32)),
        ret_mult=jnp.stack([one, (R_QK_DIM ** -0.5) * kmask]),
    )


def _silu(g):
    return g * (1.0 / (1.0 + jnp.exp(-g)))


def _attn_kernel(q_ref, k_ref, v_ref, g_ref, mk_ref, mv_ref, mbias_ref, lq1_ref, lk1_ref, lq2_ref, lk2_ref,
                 gain_ref, o_ref, m_sc, l_sc, acc_sc, g_sc, *, tk, rt):
    tq = q_ref.shape[0]
    n_kb = k_ref.shape[0] // tk
    nt_dims = (((1,), (1,)), ((), ()))
    sub = (pl.ds(0, A_QK_DIM), pl.ds(A_QK_DIM, A_QK_DIM))

    def lane_tiles(x):
        return [x[:, t * LANES:(t + 1) * LANES] for t in range(x.shape[1] // LANES)]

    def update(idx, rows, s, v_blk, first):
        tiles = lane_tiles(s)
        mx = functools.reduce(jnp.maximum, tiles)
        m_cur = jnp.max(mx, axis=-1, keepdims=True)
        if first:
            m_new = jnp.broadcast_to(m_cur, (rt, LANES))
        else:
            m_prev = m_sc[idx, rows, :]
            m_new = jnp.maximum(m_prev, m_cur)
        p_tiles = [jnp.exp2(t - m_new) for t in tiles]
        p = jnp.concatenate(p_tiles, axis=1) if len(p_tiles) > 1 else p_tiles[0]
        pv = jnp.dot(p.astype(BF16), v_blk, preferred_element_type=F32)
        l_part = functools.reduce(jnp.add, p_tiles)
        if first:
            l_sc[idx, rows, :] = l_part
            acc_sc[idx, rows, :] = pv
        else:
            alpha = jnp.exp2(m_prev - m_new)
            l_sc[idx, rows, :] = alpha * l_sc[idx, rows, :] + l_part
            acc_sc[idx, rows, :] = jnp.concatenate([alpha, alpha], axis=1) * acc_sc[idx, rows, :] + pv
        m_sc[idx, rows, :] = m_new

    def scores(rows, idx, keys):
        return lax.dot_general(q_ref[rows, sub[idx]], keys, nt_dims, preferred_element_type=F32)

    def update_fixed(idx, rows, s, v_blk):
        tiles = lane_tiles(s)
        m_ref = m_sc[idx, rows, :]
        p_tiles = [jnp.exp2(t - m_ref) for t in tiles]
        p = jnp.concatenate(p_tiles, axis=1)
        acc_sc[idx, rows, :] += jnp.dot(p.astype(BF16), v_blk, preferred_element_type=F32)
        l_sc[idx, rows, :] += functools.reduce(jnp.add, p_tiles)
        g_sc[idx, rows, :] = jnp.maximum(g_sc[idx, rows, :], functools.reduce(jnp.maximum, tiles))

    def first_step():
        mbias = mbias_ref[...] * LOG2E
        v_first = jnp.concatenate([mv_ref[...], v_ref[pl.ds(0, tk), :]], axis=0)
        for r in range(tq // rt):
            rows = pl.ds(r * rt, rt)
            for idx in range(2):
                s_meta = scores(rows, idx, mk_ref[:, sub[idx]]) + mbias
                s_tok = scores(rows, idx, k_ref[pl.ds(0, tk), sub[idx]])
                update(idx, rows, jnp.concatenate([s_meta, s_tok], axis=1), v_first, True)

    def kv_loop(step):
        def body(kb, carry):
            off = pl.multiple_of(kb * tk, tk)
            v_blk = v_ref[pl.ds(off, tk), :]
            for r in range(tq // rt):
                rows = pl.ds(r * rt, rt)
                for idx in range(2):
                    step(idx, rows, scores(rows, idx, k_ref[pl.ds(off, tk), sub[idx]]), v_blk)
            return carry

        lax.fori_loop(1, n_kb, body, 0, unroll=ATTN_UNROLL)

    def finish():
        lam = (jnp.exp(jnp.sum(lq1_ref[...] * lk1_ref[...], axis=-1, keepdims=True))
               - jnp.exp(jnp.sum(lq2_ref[...] * lk2_ref[...], axis=-1, keepdims=True)) + LAMBDA_INIT)
        inv1 = 1.0 / jnp.sum(l_sc[0], axis=-1, keepdims=True)
        inv2 = lam / jnp.sum(l_sc[1], axis=-1, keepdims=True)
        o = acc_sc[0] * inv1 - acc_sc[1] * inv2
        ms = jnp.mean(o * o, axis=-1, keepdims=True)
        y = (o * lax.rsqrt(ms + NORM_EPS)) * (gain_ref[...] * (1.0 - LAMBDA_INIT))
        o_ref[...] = (y * _silu(g_ref[...].astype(F32))).astype(o_ref.dtype)

    first_step()
    g_sc[...] = m_sc[...]
    kv_loop(update_fixed)
    finish()
    excess = jnp.max(g_sc[...] - m_sc[...])

    @pl.when(excess > ATTN_GUARD_LOG2)
    def _():
        first_step()
        kv_loop(lambda idx, rows, s, v_blk: update(idx, rows, s, v_blk, False))
        finish()


def _attention(projs, metas, mbias, lams, subln_gain, batch, seq, tq, tk):
    t = batch * seq
    tq, tk = min(tq, seq), min(tk, seq)
    nq = seq // tq
    hb = A_V_DIM
    proj_a, proj_p, _ = projs
    meta_a, meta_p, _ = metas
    q_spec = pl.BlockSpec((tq, hb), lambda b, h, i: (b * nq + i, h))
    k_spec = pl.BlockSpec((seq, hb), lambda b, h, i: (b, A_HEADS + h))
    v_spec = pl.BlockSpec((seq, hb), lambda b, h, i: (b, h))
    g_spec = pl.BlockSpec((tq, hb), lambda b, h, i: (b * nq + i, A_HEADS + h))
    mk_spec = pl.BlockSpec((CHUNK, hb), lambda b, h, i: (0, A_HEADS + h))
    mv_spec = pl.BlockSpec((CHUNK, hb), lambda b, h, i: (0, h))
    row128 = pl.BlockSpec((1, LANES), lambda b, h, i: (0, 0))
    gain_spec = pl.BlockSpec((1, A_V_DIM), lambda b, h, i: (0, 0))
    vmem = (2 * 2 * seq * hb * 2 + 2 * 3 * tq * hb * 2 + 2 * 2 * CHUNK * hb * 2
            + 2 * tq * (2 * LANES + A_V_DIM) * 4 + 6 * tq * tk * 4)
    return pl.pallas_call(
        functools.partial(_attn_kernel, tk=tk, rt=min(ATTN_RT, tq)),
        grid=(batch, A_HEADS, nq),
        in_specs=[q_spec, k_spec, v_spec, g_spec, mk_spec, mv_spec, row128, row128, row128, row128, row128,
                  gain_spec],
        out_specs=pl.BlockSpec((tq, hb), lambda b, h, i: (b * nq + i, h)),
        out_shape=jax.ShapeDtypeStruct((t, A_HEADS * A_V_DIM), BF16),
        scratch_shapes=[pltpu.VMEM((2, tq, LANES), F32), pltpu.VMEM((2, tq, LANES), F32),
                        pltpu.VMEM((2, tq, A_V_DIM), F32), pltpu.VMEM((2, tq, LANES), F32)],
        compiler_params=_params(("arbitrary", "arbitrary", "arbitrary"), vmem),
        name="diff_attention",
    )(proj_a, proj_a, proj_p, proj_p, meta_a, meta_p, mbias, *lams, subln_gain.reshape(1, A_V_DIM))


def _log_sigmoid(x):
    return jnp.minimum(x, 0.0) - jnp.log(1.0 + jnp.exp(-jnp.abs(x)))


def _ret_kernel(q_ref, k_ref, v_ref, g_ref, mk_ref, mv_ref, decf_ref, decb_ref, gain_ref, o_ref, u_sc, st_sc):
    c = RET_CHUNK
    dk, dv = R_QK_DIM, R_V_DIM
    n_chunks = q_ref.shape[0] // c
    nt_dims = (((1,), (1,)), ((), ()))
    lg_f = _log_sigmoid(decf_ref[0])
    lg_b = _log_sigmoid(decb_ref[0])
    lg_f1, lg_b1 = lg_f[:, :1], lg_b[:, :1]

    row = lax.broadcasted_iota(jnp.int32, (c, c), 0)
    col = lax.broadcasted_iota(jnp.int32, (c, c), 1)
    rel = (row - col).astype(F32)
    fwd_mask = rel >= 0
    d_both = jnp.where(fwd_mask, jnp.exp(jnp.where(fwd_mask, rel, 0.0) * lg_f1),
                       jnp.exp(jnp.where(fwd_mask, 0.0, -rel) * lg_b1))
    r_v = lax.broadcasted_iota(jnp.int32, (c, dv), 0).astype(F32)
    r_k = lax.broadcasted_iota(jnp.int32, (c, dk), 0).astype(F32)
    qd = jnp.concatenate([jnp.exp((r_v + 1.0) * lg_f), jnp.exp((c - r_v) * lg_b)], axis=1)
    kd = jnp.concatenate([jnp.exp((c - 1.0 - r_k) * lg_f1), jnp.exp(r_k * lg_b1)], axis=1)
    cd_f = jnp.exp(c * lg_f)
    cd_b = jnp.exp(c * lg_b)

    def chunk_rows(i):
        return pl.ds(pl.multiple_of(i * c, c), c)

    def loop_a(i, carry):
        rows = chunk_rows(i)
        k_blk = k_ref[rows, :].astype(F32)
        kdec = (jnp.concatenate([k_blk, k_blk], axis=1) * kd).T.astype(BF16)
        u_sc[i] = jnp.dot(kdec, v_ref[rows, :], preferred_element_type=F32)
        return carry

    lax.fori_loop(0, n_chunks, loop_a, 0, unroll=RET_UNROLL)

    r_m = lax.broadcasted_iota(jnp.int32, (CHUNK, dk), 0).astype(F32)
    mk_dec = (mk_ref[...].astype(F32) * jnp.exp((CHUNK - 1.0 - r_m) * lg_f1)).T.astype(BF16)
    state0 = jnp.dot(mk_dec, mv_ref[...], preferred_element_type=F32)

    def loop_b(i, carry):
        st_f, st_b = carry
        j = n_chunks - 1 - i
        st_sc[i, :, pl.ds(0, dv)] = st_f.astype(BF16)
        st_sc[j, :, pl.ds(dv, dv)] = st_b.astype(BF16)
        return st_f * cd_f + u_sc[i, pl.ds(0, dk), :], st_b * cd_b + u_sc[j, pl.ds(dk, dk), :]

    lax.fori_loop(0, n_chunks, loop_b, (state0, jnp.zeros((dk, dv), F32)), unroll=RET_UNROLL)

    gain = gain_ref[0]

    def loop_c(i, carry):
        rows = chunk_rows(i)
        q_blk, v_blk = q_ref[rows, :], v_ref[rows, :]
        s = lax.dot_general(q_blk, k_ref[rows, :], nt_dims, preferred_element_type=F32) * d_both
        intra = jnp.dot(s.astype(BF16), v_blk, preferred_element_type=F32)
        inter = jnp.dot(q_blk, st_sc[i], preferred_element_type=F32) * qd
        ro = intra + (inter[:, :dv] + inter[:, dv:])
        ms = jnp.mean(ro * ro, axis=-1, keepdims=True)
        y = (ro * lax.rsqrt(ms + NORM_EPS)) * gain
        o_ref[rows, :] = (y * _silu(g_ref[rows, :].astype(F32))).astype(o_ref.dtype)
        return carry

    lax.fori_loop(0, n_chunks, loop_c, 0, unroll=RET_UNROLL)


def _retention(projs, metas, dec_f, dec_b, ret_gain, batch, seq):
    t = batch * seq
    kb, vb = R_QK_DIM, R_V_DIM
    _, proj_p, proj_r = projs
    _, meta_p, meta_r = metas
    q_spec = pl.BlockSpec((seq, kb), lambda b, h: (b, h))
    k_spec = pl.BlockSpec((seq, kb), lambda b, h: (b, R_HEADS + h))
    v_spec = pl.BlockSpec((seq, vb), lambda b, h: (b, 2 * A_HEADS + h))
    g_spec = pl.BlockSpec((seq, vb), lambda b, h: (b, 2 * A_HEADS + R_HEADS + h))
    mk_spec = pl.BlockSpec((CHUNK, kb), lambda b, h: (0, R_HEADS + h))
    mv_spec = pl.BlockSpec((CHUNK, vb), lambda b, h: (0, 2 * A_HEADS + h))
    head_spec = pl.BlockSpec((1, 1, vb), lambda b, h: (h, 0, 0))
    n_chunks = seq // RET_CHUNK
    vmem = 2 * seq * (2 * kb + 3 * vb) * 2 + n_chunks * 2 * kb * vb * (4 + 2) + 64 * RET_CHUNK * vb * 4
    return pl.pallas_call(
        _ret_kernel,
        grid=(batch, R_HEADS),
        in_specs=[q_spec, k_spec, v_spec, g_spec, mk_spec, mv_spec, head_spec, head_spec, head_spec],
        out_specs=pl.BlockSpec((seq, vb), lambda b, h: (b, h)),
        out_shape=jax.ShapeDtypeStruct((t, R_HEADS * R_V_DIM), BF16),
        scratch_shapes=[pltpu.VMEM((n_chunks, 2 * kb, vb), F32), pltpu.VMEM((n_chunks, kb, 2 * vb), BF16)],
        compiler_params=_params(("arbitrary", "arbitrary"), vmem),
        name="retention",
    )(proj_r, proj_r, proj_p, proj_p, meta_r, meta_p, dec_f, dec_b, ret_gain)


def _out_kernel(a_ref, r_ref, w_ref, x_ref, g_ref, o_ref):
    half = a_ref.shape[1]
    h = x_ref[...] + jnp.dot(a_ref[...], w_ref[pl.ds(0, half), :], preferred_element_type=F32)
    h = h + jnp.dot(r_ref[...], w_ref[pl.ds(half, half), :], preferred_element_type=F32)
    ms = jnp.mean(h * h, axis=-1, keepdims=True)
    o_ref[...] = (h * lax.rsqrt(ms + NORM_EPS)) * g_ref[...]


def _out_proj(ao, ro, w_bf16, x2d, gain, tm):
    t, d = x2d.shape
    tm = min(tm, t)
    half = d // 2
    vmem = d * d * 2 + 2 * 2 * tm * half * 2 + 2 * 2 * tm * d * 4 + 2 * tm * d * 4
    return pl.pallas_call(
        _out_kernel,
        grid=(t // tm,),
        in_specs=[
            pl.BlockSpec((tm, half), lambda i: (i, 0)),
            pl.BlockSpec((tm, half), lambda i: (i, 0)),
            pl.BlockSpec((d, d), lambda i: (0, 0), pipeline_mode=pl.Buffered(1)),
            pl.BlockSpec((tm, d), lambda i: (i, 0)),
            pl.BlockSpec((1, d), lambda i: (0, 0)),
        ],
        out_specs=pl.BlockSpec((tm, d), lambda i: (i, 0)),
        out_shape=jax.ShapeDtypeStruct((t, d), F32),
        compiler_params=_params(("arbitrary",), vmem),
        name="out_proj",
    )(ao, ro, w_bf16, x2d, gain.reshape(1, d))


def _trunk(x, shared):
    batch, seq, d = x.shape
    x2d = x.reshape(batch * seq, d)
    u = _rmsnorm(x2d, shared["norm_gain"], NORM_TM)
    projs = _project(u, shared["w_in"], _rotary_tables(jnp.arange(seq, dtype=jnp.int32) + N_META), seq)
    ao = _attention(projs, shared["proj_meta"], shared["mbias"], shared["lams"], shared["subln_gain"],
                    batch, seq, ATTN_TQ, ATTN_TK)
    ro = _retention(projs, shared["proj_meta"], shared["dec_f"], shared["dec_b"], shared["ret_gain"], batch, seq)
    y = _out_proj(ao, ro, shared["w_out"], x2d, shared["final_gain"], OUT_TM)
    return y.reshape(batch, seq, d)


def kernel(x_prompt, x_sample, meta_tokens, norm_gain, w_in, lambda_q1, lambda_k1, lambda_q2, lambda_k2,
           attn_subln_gain, ret_decay_fwd, ret_decay_bwd, ret_norm_gain, w_out, final_norm_gain):
    assert norm_gain.shape[0] == 1, "single-layer trunk"
    d = x_prompt.shape[-1]
    assert d == D_MODEL and w_in.shape[1:] == (D_MODEL, PROJ_OUT) and w_out.shape[1:] == (D_MODEL, D_MODEL)
    w_in_bf16 = w_in[0].astype(BF16)
    w_out_bf16 = w_out[0].astype(BF16)

    meta_pos = jnp.arange(CHUNK, dtype=jnp.int32) - (CHUNK - N_META)
    meta_rows = jnp.concatenate([jnp.zeros((CHUNK - N_META, d), F32), meta_tokens.astype(F32)], axis=0)
    u_meta = _rmsnorm(meta_rows, norm_gain[0], CHUNK)
    proj_meta = _project(u_meta, w_in_bf16, _rotary_tables(meta_pos), CHUNK)
    mbias = jnp.where(meta_pos >= 0, 0.0, NEG_BIG).astype(F32).reshape(1, CHUNK)

    def per_head(v):
        return jnp.broadcast_to(v.astype(F32)[:, None, None], (R_HEADS, 1, R_V_DIM))

    shared = dict(
        norm_gain=norm_gain[0], w_in=w_in_bf16, w_out=w_out_bf16, proj_meta=proj_meta, mbias=mbias,
        lams=tuple(v[0].astype(F32).reshape(1, A_QK_DIM) for v in (lambda_q1, lambda_k1, lambda_q2, lambda_k2)),
        subln_gain=attn_subln_gain[0].astype(F32),
        dec_f=per_head(ret_decay_fwd[0]), dec_b=per_head(ret_decay_bwd[0]),
        ret_gain=ret_norm_gain[0].astype(F32).reshape(R_HEADS, 1, R_V_DIM),
        final_gain=final_norm_gain.astype(F32),
    )
    return (_trunk(x_prompt, shared), _trunk(x_sample, shared))
```

```python
import functools
import math

import jax
import jax.numpy as jnp
from jax import lax
from jax.experimental import pallas as pl
from jax.experimental.pallas import tpu as pltpu

F32 = jnp.float32
BF16 = jnp.bfloat16

D_MODEL = 4096
N_META = 16
CHUNK = 128
A_HEADS = 8
A_QK_DIM = 128
A_V_DIM = 256
A_ROT_DIM = A_QK_DIM // 4
ROPE_THETA = 500000.0
R_HEADS = 8
R_QK_DIM = 128
R_V_DIM = 256
RET_THETA = 10000.0
NORM_EPS = 1e-6
NEG_BIG = -1e30
LAMBDA_INIT = 0.8 - 0.6 * math.exp(-0.3 * 0)

AQ_OFF = 0
AV_OFF = AQ_OFF + 2 * (2 * A_HEADS * A_QK_DIM)
RQ_OFF = AV_OFF + 2 * (A_HEADS * A_V_DIM)
RV_OFF = RQ_OFF + 2 * (R_HEADS * R_QK_DIM)
PROJ_OUT = RV_OFF + 2 * (R_HEADS * R_V_DIM)

LANES = 128
BF16_SUBLANES = 16
LOG2E = math.log2(math.e)
ATTN_SCORE_SCALE = (A_QK_DIM ** -0.5) * LOG2E
V7X_VMEM_BYTES = 64 * 1024 * 1024
V7X_VMEM_BUDGET = V7X_VMEM_BYTES - 4 * 1024 * 1024

PROJ_TM = 1024
PROJ_TN = 1024
PROJ_RM = 256
ATTN_TQ = 1024
ATTN_TK = 512
ATTN_RT = 128
ATTN_GUARD_LOG2 = 64.0
ATTN_UNROLL = 5
OUT_TM = 256
RET_CHUNK = 256
RET_UNROLL = 4
NORM_TM = 512


def _params(sem, vmem_bytes):
    return pltpu.CompilerParams(
        dimension_semantics=sem, vmem_limit_bytes=min(int(vmem_bytes), V7X_VMEM_BUDGET)
    )


def _rmsnorm_kernel(x_ref, g_ref, o_ref):
    x = x_ref[...]
    ms = jnp.mean(x * x, axis=-1, keepdims=True)
    y = x * lax.rsqrt(ms + NORM_EPS)
    o_ref[...] = (y * g_ref[...]).astype(o_ref.dtype)


def _rmsnorm(x2d, gain, tm):
    t, d = x2d.shape
    tm = min(tm, t)
    return pl.pallas_call(
        _rmsnorm_kernel,
        grid=(t // tm,),
        in_specs=[pl.BlockSpec((tm, d), lambda i: (i, 0)), pl.BlockSpec((1, d), lambda i: (0, 0))],
        out_specs=pl.BlockSpec((tm, d), lambda i: (i, 0)),
        out_shape=jax.ShapeDtypeStruct((t, d), BF16),
        compiler_params=_params(("arbitrary",), 2 * tm * d * (4 + 2) + 4 * tm * d * 4),
        name="rmsnorm_in",
    )(x2d, gain.reshape(1, d))


def _proj_plain_kernel(u_ref, w_ref, o_ref, *, rm):
    for c in range(u_ref.shape[0] // rm):
        rows = pl.ds(c * rm, rm)
        o_ref[rows, :] = jnp.dot(u_ref[rows, :], w_ref[...], preferred_element_type=F32).astype(o_ref.dtype)


def _proj_plain_side_kernel(u_ref, w_ref, xs_ref, gs_ref, o_ref, us_ref, *, rm):
    _proj_plain_kernel(u_ref, w_ref, o_ref, rm=rm)
    _rmsnorm_kernel(xs_ref, gs_ref, us_ref)


def _proj_rot_kernel(u_ref, w_ref, cos_ref, sin1_ref, sin2_ref, mult_ref, o_ref, *, rm, shift1, shift2):
    for c in range(u_ref.shape[0] // rm):
        rows = pl.ds(c * rm, rm)
        acc = jnp.dot(u_ref[rows, :], w_ref[...], preferred_element_type=F32)
        cos, sin1, sin2, mult = cos_ref[rows, :], sin1_ref[rows, :], sin2_ref[rows, :], mult_ref[rows, :]
        for g in range(acc.shape[1] // LANES):
            x = acc[:, g * LANES:(g + 1) * LANES]
            y = x * cos + pltpu.roll(x, shift1, 1) * sin1 + pltpu.roll(x, shift2, 1) * sin2
            o_ref[rows, g * LANES:(g + 1) * LANES] = (y * mult).astype(o_ref.dtype)


def _proj_plain(u, w_bf16, tm, tn, rm, side=None):
    t, d = u.shape
    tm = min(tm, t)
    n_a = (RQ_OFF - AV_OFF) // tn
    n_cols = (RQ_OFF - AV_OFF) + (PROJ_OUT - RV_OFF)
    vmem = 2 * tm * d * 2 + 2 * d * tn * 2 + 2 * tm * tn * 2 + 2 * min(rm, tm) * tn * 4
    n_j = n_cols // tn
    in_specs = [pl.BlockSpec((tm, d), lambda i, j: (i, 0)),
                pl.BlockSpec((d, tn), lambda i, j: (0, jnp.where(j < n_a, AV_OFF // tn + j, RV_OFF // tn + j - n_a)))]
    out_specs = pl.BlockSpec((tm, tn), lambda i, j: (i, j))
    out_shape = jax.ShapeDtypeStruct((t, n_cols), BF16)
    args = (u, w_bf16)
    body = _proj_plain_kernel
    if side is not None:
        body = _proj_plain_side_kernel
        xs, gain = side
        rs = xs.shape[0] // ((t // tm) * n_j)
        side_spec = pl.BlockSpec((rs, d), lambda i, j: (i * n_j + j, 0))
        in_specs += [side_spec, pl.BlockSpec((1, d), lambda i, j: (0, 0))]
        out_specs = (out_specs, side_spec)
        out_shape = (out_shape, jax.ShapeDtypeStruct(xs.shape, BF16))
        args += (xs, gain.reshape(1, d))
        vmem += 2 * rs * d * (4 + 2) + 4 * rs * d * 4
    return pl.pallas_call(
        functools.partial(body, rm=min(rm, tm)),
        grid=(t // tm, n_j),
        in_specs=in_specs,
        out_specs=out_specs,
        out_shape=out_shape,
        compiler_params=_params(("arbitrary", "arbitrary"), vmem),
        name="in_proj_plain",
    )(*args)


def _proj_rot(u, w_bf16, tables, mults, col_off, n_cols, shifts, rows_per_seq, tm, tn, rm, name):
    t, d = u.shape
    tm = min(tm, rows_per_seq)
    n_seq_tiles = rows_per_seq // tm
    n_j = n_cols // tn
    tab_spec = pl.BlockSpec((tm, LANES), lambda i, j: (i % n_seq_tiles, 0))
    mult_spec = pl.BlockSpec((None, tm, LANES), lambda i, j: (j // (n_j // 2), i % n_seq_tiles, 0))
    vmem = 2 * tm * d * 2 + 2 * d * tn * 2 + 2 * tm * tn * 2 + 2 * 4 * tm * LANES * 4 + 3 * min(rm, tm) * tn * 4
    return pl.pallas_call(
        functools.partial(_proj_rot_kernel, rm=min(rm, tm), shift1=shifts[0], shift2=shifts[1]),
        grid=(t // tm, n_j),
        in_specs=[pl.BlockSpec((tm, d), lambda i, j: (i, 0)),
                  pl.BlockSpec((d, tn), lambda i, j: (0, col_off // tn + j)),
                  tab_spec, tab_spec, tab_spec, mult_spec],
        out_specs=pl.BlockSpec((tm, tn), lambda i, j: (i, j)),
        out_shape=jax.ShapeDtypeStruct((t, n_cols), BF16),
        compiler_params=_params(("arbitrary", "arbitrary"), vmem),
        name=name,
    )(u, w_bf16, *tables, mults)


def _side_rmsnorm_fits(t_main, t_side):
    steps = (t_main // min(PROJ_TM, t_main)) * ((RQ_OFF - AV_OFF + PROJ_OUT - RV_OFF) // PROJ_TN)
    return t_side % steps == 0 and (t_side // steps) % BF16_SUBLANES == 0


def _project(u, w_bf16, tabs, rows_per_seq, side=None):
    proj_a = _proj_rot(u, w_bf16, tabs["attn"], tabs["attn_mult"], AQ_OFF, AV_OFF - AQ_OFF,
                       (LANES - A_ROT_DIM // 2, A_ROT_DIM // 2), rows_per_seq, PROJ_TM, PROJ_TN, PROJ_RM, "in_proj_attn")
    proj_p = _proj_plain(u, w_bf16, PROJ_TM, PROJ_TN, PROJ_RM, side)
    u_side = None
    if side is not None:
        proj_p, u_side = proj_p
    proj_r = _proj_rot(u, w_bf16, tabs["ret"], tabs["ret_mult"], RQ_OFF, RV_OFF - RQ_OFF,
                       (R_QK_DIM // 2, R_QK_DIM // 2), rows_per_seq, PROJ_TM, PROJ_TN, PROJ_RM, "in_proj_ret")
    return (proj_a, proj_p, proj_r), u_side


def _rotary_tables(pos):
    n = pos.shape[0]
    posf = pos.astype(F32)[:, None]
    half = A_ROT_DIM // 2
    inv = jnp.power(jnp.float32(ROPE_THETA), -jnp.arange(half, dtype=F32) * 2.0 / A_ROT_DIM)
    ang = posf * inv[None, :]
    cos, sin = jnp.cos(ang), jnp.sin(ang)
    ones = jnp.ones((n, LANES - A_ROT_DIM), F32)
    zeros_h = jnp.zeros((n, half), F32)
    zeros_r = jnp.zeros((n, LANES - A_ROT_DIM), F32)
    cos_a = jnp.concatenate([cos, cos, ones], axis=1)
    sin_hi = jnp.concatenate([-sin, zeros_h, zeros_r], axis=1)
    sin_lo = jnp.concatenate([zeros_h, sin, zeros_r], axis=1)
    half_r = R_QK_DIM // 2
    inv_r = jnp.power(jnp.float32(RET_THETA), -jnp.arange(half_r, dtype=F32) * 2.0 / R_QK_DIM)
    ang_r = posf * inv_r[None, :]
    cos_r = jnp.concatenate([jnp.cos(ang_r), jnp.cos(ang_r)], axis=1)
    sin_r = jnp.concatenate([-jnp.sin(ang_r), jnp.sin(ang_r)], axis=1)
    one = jnp.ones((n, LANES), F32)
    kmask = jnp.broadcast_to((pos >= 0).astype(F32)[:, None], (n, LANES))
    return dict(
        attn=(cos_a, sin_hi, sin_lo),
        attn_mult=jnp.stack([one * ATTN_SCORE_SCALE, one]),
        ret=(cos_r, sin_r, jnp.zeros((n, LANES), F32)),
        ret_mult=jnp.stack([one, (R_QK_DIM ** -0.5) * kmask]),
    )


def _silu(g):
    h = 0.5 * g
    return h + h * jnp.tanh(h)


def _attn_kernel(q_ref, k_ref, v_ref, g_ref, mk_ref, mv_ref, mbias_ref, lq1_ref, lk1_ref, lq2_ref, lk2_ref,
                 gain_ref, o_ref, m_sc, l_sc, acc_sc, g_sc, *, tk, rt):
    tq = q_ref.shape[0]
    n_kb = k_ref.shape[0] // tk
    nt_dims = (((1,), (1,)), ((), ()))
    sub = (pl.ds(0, A_QK_DIM), pl.ds(A_QK_DIM, A_QK_DIM))

    def lane_tiles(x):
        return [x[:, t * LANES:(t + 1) * LANES] for t in range(x.shape[1] // LANES)]

    def update(idx, rows, s, v_blk, first):
        tiles = lane_tiles(s)
        mx = functools.reduce(jnp.maximum, tiles)
        m_cur = jnp.max(mx, axis=-1, keepdims=True)
        if first:
            m_new = jnp.broadcast_to(m_cur, (rt, LANES))
        else:
            m_prev = m_sc[idx, rows, :]
            m_new = jnp.maximum(m_prev, m_cur)
        p_tiles = [jnp.exp2(t - m_new) for t in tiles]
        p = jnp.concatenate(p_tiles, axis=1) if len(p_tiles) > 1 else p_tiles[0]
        pv = jnp.dot(p.astype(BF16), v_blk, preferred_element_type=F32)
        l_part = functools.reduce(jnp.add, p_tiles)
        if first:
            l_sc[idx, rows, :] = l_part
            acc_sc[idx, rows, :] = pv
        else:
            alpha = jnp.exp2(m_prev - m_new)
            l_sc[idx, rows, :] = alpha * l_sc[idx, rows, :] + l_part
            acc_sc[idx, rows, :] = jnp.concatenate([alpha, alpha], axis=1) * acc_sc[idx, rows, :] + pv
        m_sc[idx, rows, :] = m_new

    def scores(rows, idx, keys):
        return lax.dot_general(q_ref[rows, sub[idx]], keys, nt_dims, preferred_element_type=F32)

    def update_fixed(idx, rows, s, v_blk):
        tiles = lane_tiles(s)
        m_ref = m_sc[idx, rows, :]
        p_tiles = [jnp.exp2(t - m_ref) for t in tiles]
        p = jnp.concatenate(p_tiles, axis=1)
        acc_sc[idx, rows, :] += jnp.dot(p.astype(BF16), v_blk, preferred_element_type=F32)
        l_sc[idx, rows, :] += functools.reduce(jnp.add, p_tiles)
        g_sc[idx, rows, :] = jnp.maximum(g_sc[idx, rows, :], functools.reduce(jnp.maximum, tiles))

    def first_step():
        mbias = mbias_ref[...] * LOG2E
        v_first = jnp.concatenate([mv_ref[...], v_ref[pl.ds(0, tk), :]], axis=0)
        for r in range(tq // rt):
            rows = pl.ds(r * rt, rt)
            for idx in range(2):
                s_meta = scores(rows, idx, mk_ref[:, sub[idx]]) + mbias
                s_tok = scores(rows, idx, k_ref[pl.ds(0, tk), sub[idx]])
                update(idx, rows, jnp.concatenate([s_meta, s_tok], axis=1), v_first, True)

    def kv_loop(step):
        def body(kb, carry):
            off = pl.multiple_of(kb * tk, tk)
            v_blk = v_ref[pl.ds(off, tk), :]
            for r in range(tq // rt):
                rows = pl.ds(r * rt, rt)
                for idx in range(2):
                    step(idx, rows, scores(rows, idx, k_ref[pl.ds(off, tk), sub[idx]]), v_blk)
            return carry

        lax.fori_loop(1, n_kb, body, 0, unroll=ATTN_UNROLL)

    def finish():
        lam = (jnp.exp(jnp.sum(lq1_ref[...] * lk1_ref[...], axis=-1, keepdims=True))
               - jnp.exp(jnp.sum(lq2_ref[...] * lk2_ref[...], axis=-1, keepdims=True)) + LAMBDA_INIT)
        inv1 = 1.0 / jnp.sum(l_sc[0], axis=-1, keepdims=True)
        inv2 = lam / jnp.sum(l_sc[1], axis=-1, keepdims=True)
        o = acc_sc[0] * inv1 - acc_sc[1] * inv2
        ms = jnp.mean(o * o, axis=-1, keepdims=True)
        y = (o * lax.rsqrt(ms + NORM_EPS)) * (gain_ref[...] * (1.0 - LAMBDA_INIT))
        o_ref[...] = (y * _silu(g_ref[...].astype(F32))).astype(o_ref.dtype)

    first_step()
    g_sc[...] = m_sc[...]
    kv_loop(update_fixed)
    finish()
    excess = jnp.max(g_sc[...] - m_sc[...])

    @pl.when(excess > ATTN_GUARD_LOG2)
    def _():
        first_step()
        kv_loop(lambda idx, rows, s, v_blk: update(idx, rows, s, v_blk, False))
        finish()


def _attention(projs, metas, mbias, lams, subln_gain, batch, seq, tq, tk):
    t = batch * seq
    tq, tk = min(tq, seq), min(tk, seq)
    nq = seq // tq
    hb = A_V_DIM
    proj_a, proj_p, _ = projs
    meta_a, meta_p, _ = metas
    q_spec = pl.BlockSpec((tq, hb), lambda b, h, i: (b * nq + i, h))
    k_spec = pl.BlockSpec((seq, hb), lambda b, h, i: (b, A_HEADS + h))
    v_spec = pl.BlockSpec((seq, hb), lambda b, h, i: (b, h))
    g_spec = pl.BlockSpec((tq, hb), lambda b, h, i: (b * nq + i, A_HEADS + h))
    mk_spec = pl.BlockSpec((CHUNK, hb), lambda b, h, i: (0, A_HEADS + h))
    mv_spec = pl.BlockSpec((CHUNK, hb), lambda b, h, i: (0, h))
    row128 = pl.BlockSpec((1, LANES), lambda b, h, i: (0, 0))
    gain_spec = pl.BlockSpec((1, A_V_DIM), lambda b, h, i: (0, 0))
    vmem = (2 * 2 * seq * hb * 2 + 2 * 3 * tq * hb * 2 + 2 * 2 * CHUNK * hb * 2
            + 2 * tq * (2 * LANES + A_V_DIM) * 4 + 6 * tq * tk * 4)
    return pl.pallas_call(
        functools.partial(_attn_kernel, tk=tk, rt=min(ATTN_RT, tq)),
        grid=(batch, A_HEADS, nq),
        in_specs=[q_spec, k_spec, v_spec, g_spec, mk_spec, mv_spec, row128, row128, row128, row128, row128,
                  gain_spec],
        out_specs=pl.BlockSpec((tq, hb), lambda b, h, i: (b * nq + i, h)),
        out_shape=jax.ShapeDtypeStruct((t, A_HEADS * A_V_DIM), BF16),
        scratch_shapes=[pltpu.VMEM((2, tq, LANES), F32), pltpu.VMEM((2, tq, LANES), F32),
                        pltpu.VMEM((2, tq, A_V_DIM), F32), pltpu.VMEM((2, tq, LANES), F32)],
        compiler_params=_params(("arbitrary", "arbitrary", "arbitrary"), vmem),
        name="diff_attention",
    )(proj_a, proj_a, proj_p, proj_p, meta_a, meta_p, mbias, *lams, subln_gain.reshape(1, A_V_DIM))


def _log_sigmoid(x):
    return jnp.minimum(x, 0.0) - jnp.log(1.0 + jnp.exp(-jnp.abs(x)))


def _ret_kernel(q_ref, k_ref, v_ref, g_ref, mk_ref, mv_ref, decf_ref, decb_ref, gain_ref, o_ref, u_sc, st_sc):
    c = RET_CHUNK
    dk, dv = R_QK_DIM, R_V_DIM
    n_chunks = q_ref.shape[0] // c
    nt_dims = (((1,), (1,)), ((), ()))
    lg_f = _log_sigmoid(decf_ref[0])
    lg_b = _log_sigmoid(decb_ref[0])
    lg_f1, lg_b1 = lg_f[:, :1], lg_b[:, :1]

    row = lax.broadcasted_iota(jnp.int32, (c, c), 0)
    col = lax.broadcasted_iota(jnp.int32, (c, c), 1)
    rel = (row - col).astype(F32)
    fwd_mask = rel >= 0
    d_both = jnp.where(fwd_mask, jnp.exp(jnp.where(fwd_mask, rel, 0.0) * lg_f1),
                       jnp.exp(jnp.where(fwd_mask, 0.0, -rel) * lg_b1))
    r_v = lax.broadcasted_iota(jnp.int32, (c, dv), 0).astype(F32)
    r_k = lax.broadcasted_iota(jnp.int32, (c, dk), 0).astype(F32)
    qd = jnp.concatenate([jnp.exp((r_v + 1.0) * lg_f), jnp.exp((c - r_v) * lg_b)], axis=1)
    kd = jnp.concatenate([jnp.exp((c - 1.0 - r_k) * lg_f1), jnp.exp(r_k * lg_b1)], axis=1)
    cd_f = jnp.exp(c * lg_f)
    cd_b = jnp.exp(c * lg_b)

    def chunk_rows(i):
        return pl.ds(pl.multiple_of(i * c, c), c)

    def loop_a(i, carry):
        rows = chunk_rows(i)
        k_blk = k_ref[rows, :].astype(F32)
        kdec = (jnp.concatenate([k_blk, k_blk], axis=1) * kd).T.astype(BF16)
        u_sc[i] = jnp.dot(kdec, v_ref[rows, :], preferred_element_type=F32)
        return carry

    lax.fori_loop(0, n_chunks, loop_a, 0, unroll=RET_UNROLL)

    r_m = lax.broadcasted_iota(jnp.int32, (CHUNK, dk), 0).astype(F32)
    mk_dec = (mk_ref[...].astype(F32) * jnp.exp((CHUNK - 1.0 - r_m) * lg_f1)).T.astype(BF16)
    state0 = jnp.dot(mk_dec, mv_ref[...], preferred_element_type=F32)

    def loop_b(i, carry):
        st_f, st_b = carry
        j = n_chunks - 1 - i
        st_sc[i, :, pl.ds(0, dv)] = st_f.astype(BF16)
        st_sc[j, :, pl.ds(dv, dv)] = st_b.astype(BF16)
        return st_f * cd_f + u_sc[i, pl.ds(0, dk), :], st_b * cd_b + u_sc[j, pl.ds(dk, dk), :]

    lax.fori_loop(0, n_chunks, loop_b, (state0, jnp.zeros((dk, dv), F32)), unroll=RET_UNROLL)

    gain = gain_ref[0]

    def loop_c(i, carry):
        rows = chunk_rows(i)
        q_blk, v_blk = q_ref[rows, :], v_ref[rows, :]
        s = lax.dot_general(q_blk, k_ref[rows, :], nt_dims, preferred_element_type=F32) * d_both
        intra = jnp.dot(s.astype(BF16), v_blk, preferred_element_type=F32)
        inter = jnp.dot(q_blk, st_sc[i], preferred_element_type=F32) * qd
        ro = intra + (inter[:, :dv] + inter[:, dv:])
        ms = jnp.mean(ro * ro, axis=-1, keepdims=True)
        y = (ro * lax.rsqrt(ms + NORM_EPS)) * gain
        o_ref[rows, :] = (y * _silu(g_ref[rows, :].astype(F32))).astype(o_ref.dtype)
        return carry

    lax.fori_loop(0, n_chunks, loop_c, 0, unroll=RET_UNROLL)


def _retention(projs, metas, dec_f, dec_b, ret_gain, batch, seq):
    t = batch * seq
    kb, vb = R_QK_DIM, R_V_DIM
    _, proj_p, proj_r = projs
    _, meta_p, meta_r = metas
    q_spec = pl.BlockSpec((seq, kb), lambda b, h: (b, h))
    k_spec = pl.BlockSpec((seq, kb), lambda b, h: (b, R_HEADS + h))
    v_spec = pl.BlockSpec((seq, vb), lambda b, h: (b, 2 * A_HEADS + h))
    g_spec = pl.BlockSpec((seq, vb), lambda b, h: (b, 2 * A_HEADS + R_HEADS + h))
    mk_spec = pl.BlockSpec((CHUNK, kb), lambda b, h: (0, R_HEADS + h))
    mv_spec = pl.BlockSpec((CHUNK, vb), lambda b, h: (0, 2 * A_HEADS + h))
    head_spec = pl.BlockSpec((1, 1, vb), lambda b, h: (h, 0, 0))
    n_chunks = seq // RET_CHUNK
    vmem = 2 * seq * (2 * kb + 3 * vb) * 2 + n_chunks * 2 * kb * vb * (4 + 2) + 64 * RET_CHUNK * vb * 4
    return pl.pallas_call(
        _ret_kernel,
        grid=(batch, R_HEADS),
        in_specs=[q_spec, k_spec, v_spec, g_spec, mk_spec, mv_spec, head_spec, head_spec, head_spec],
        out_specs=pl.BlockSpec((seq, vb), lambda b, h: (b, h)),
        out_shape=jax.ShapeDtypeStruct((t, R_HEADS * R_V_DIM), BF16),
        scratch_shapes=[pltpu.VMEM((n_chunks, 2 * kb, vb), F32), pltpu.VMEM((n_chunks, kb, 2 * vb), BF16)],
        compiler_params=_params(("arbitrary", "arbitrary"), vmem),
        name="retention",
    )(proj_r, proj_r, proj_p, proj_p, meta_r, meta_p, dec_f, dec_b, ret_gain)


def _out_kernel(a_ref, r_ref, w_ref, x_ref, g_ref, o_ref):
    half = a_ref.shape[1]
    h = x_ref[...] + jnp.dot(a_ref[...], w_ref[pl.ds(0, half), :], preferred_element_type=F32)
    h = h + jnp.dot(r_ref[...], w_ref[pl.ds(half, half), :], preferred_element_type=F32)
    ms = jnp.mean(h * h, axis=-1, keepdims=True)
    o_ref[...] = (h * lax.rsqrt(ms + NORM_EPS)) * g_ref[...]


def _out_proj(ao, ro, w_bf16, x2d, gain, tm):
    t, d = x2d.shape
    tm = min(tm, t)
    half = d // 2
    vmem = d * d * 2 + 2 * 2 * tm * half * 2 + 2 * 2 * tm * d * 4 + 2 * tm * d * 4
    return pl.pallas_call(
        _out_kernel,
        grid=(t // tm,),
        in_specs=[
            pl.BlockSpec((tm, half), lambda i: (i, 0)),
            pl.BlockSpec((tm, half), lambda i: (i, 0)),
            pl.BlockSpec((d, d), lambda i: (0, 0), pipeline_mode=pl.Buffered(1)),
            pl.BlockSpec((tm, d), lambda i: (i, 0)),
            pl.BlockSpec((1, d), lambda i: (0, 0)),
        ],
        out_specs=pl.BlockSpec((tm, d), lambda i: (i, 0)),
        out_shape=jax.ShapeDtypeStruct((t, d), F32),
        compiler_params=_params(("arbitrary",), vmem),
        name="out_proj",
    )(ao, ro, w_bf16, x2d, gain.reshape(1, d))


def _trunk(x, shared, u=None, side_x=None):
    batch, seq, d = x.shape
    x2d = x.reshape(batch * seq, d)
    if u is None:
        u = _rmsnorm(x2d, shared["norm_gain"], NORM_TM)
    side = None if side_x is None else (side_x, shared["norm_gain"])
    projs, u_side = _project(u, shared["w_in"], _rotary_tables(jnp.arange(seq, dtype=jnp.int32) + N_META), seq, side)
    ao = _attention(projs, shared["proj_meta"], shared["mbias"], shared["lams"], shared["subln_gain"],
                    batch, seq, ATTN_TQ, ATTN_TK)
    ro = _retention(projs, shared["proj_meta"], shared["dec_f"], shared["dec_b"], shared["ret_gain"], batch, seq)
    y = _out_proj(ao, ro, shared["w_out"], x2d, shared["final_gain"], OUT_TM)
    return y.reshape(batch, seq, d), u_side


def kernel(x_prompt, x_sample, meta_tokens, norm_gain, w_in, lambda_q1, lambda_k1, lambda_q2, lambda_k2,
           attn_subln_gain, ret_decay_fwd, ret_decay_bwd, ret_norm_gain, w_out, final_norm_gain):
    assert norm_gain.shape[0] == 1, "single-layer trunk"
    d = x_prompt.shape[-1]
    assert d == D_MODEL and w_in.shape[1:] == (D_MODEL, PROJ_OUT) and w_out.shape[1:] == (D_MODEL, D_MODEL)
    w_in_bf16 = w_in[0].astype(BF16)
    w_out_bf16 = w_out[0].astype(BF16)

    meta_pos = jnp.arange(CHUNK, dtype=jnp.int32) - (CHUNK - N_META)
    meta_rows = jnp.concatenate([jnp.zeros((CHUNK - N_META, d), F32), meta_tokens.astype(F32)], axis=0)
    u_meta = _rmsnorm(meta_rows, norm_gain[0], CHUNK)
    proj_meta, _ = _project(u_meta, w_in_bf16, _rotary_tables(meta_pos), CHUNK)
    mbias = jnp.where(meta_pos >= 0, 0.0, NEG_BIG).astype(F32).reshape(1, CHUNK)

    def per_head(v):
        return jnp.broadcast_to(v.astype(F32)[:, None, None], (R_HEADS, 1, R_V_DIM))

    shared = dict(
        norm_gain=norm_gain[0], w_in=w_in_bf16, w_out=w_out_bf16, proj_meta=proj_meta, mbias=mbias,
        lams=tuple(v[0].astype(F32).reshape(1, A_QK_DIM) for v in (lambda_q1, lambda_k1, lambda_q2, lambda_k2)),
        subln_gain=attn_subln_gain[0].astype(F32),
        dec_f=per_head(ret_decay_fwd[0]), dec_b=per_head(ret_decay_bwd[0]),
        ret_gain=ret_norm_gain[0].astype(F32).reshape(R_HEADS, 1, R_V_DIM),
        final_gain=final_norm_gain.astype(F32),
    )
    xs2d = x_sample.reshape(-1, d)
    side_x = xs2d if _side_rmsnorm_fits(x_prompt.shape[0] * x_prompt.shape[1], xs2d.shape[0]) else None
    y_prompt, u_sample = _trunk(x_prompt, shared, side_x=side_x)
    y_sample, _ = _trunk(x_sample, shared, u=u_sample)
    return (y_prompt, y_sample)
```

```python
import functools
import math

import jax
import jax.numpy as jnp
from jax import lax
from jax.experimental import pallas as pl
from jax.experimental.pallas import tpu as pltpu

F32 = jnp.float32
BF16 = jnp.bfloat16

D_MODEL = 4096
N_META = 16
CHUNK = 128
A_HEADS = 8
A_QK_DIM = 128
A_V_DIM = 256
A_ROT_DIM = A_QK_DIM // 4
ROPE_THETA = 500000.0
R_HEADS = 8
R_QK_DIM = 128
R_V_DIM = 256
RET_THETA = 10000.0
NORM_EPS = 1e-6
NEG_BIG = -1e30
LAMBDA_INIT = 0.8 - 0.6 * math.exp(-0.3 * 0)

AQ_OFF = 0
AV_OFF = AQ_OFF + 2 * (2 * A_HEADS * A_QK_DIM)
RQ_OFF = AV_OFF + 2 * (A_HEADS * A_V_DIM)
RV_OFF = RQ_OFF + 2 * (R_HEADS * R_QK_DIM)
PROJ_OUT = RV_OFF + 2 * (R_HEADS * R_V_DIM)

LANES = 128
BF16_SUBLANES = 16
LOG2E = math.log2(math.e)
F32_MAX = float(jnp.finfo(jnp.float32).max)
ATTN_SCORE_SCALE = (A_QK_DIM ** -0.5) * LOG2E
V7X_VMEM_BYTES = 64 * 1024 * 1024
V7X_VMEM_BUDGET = V7X_VMEM_BYTES - 4 * 1024 * 1024

PROJ_TM = 1024
PROJ_TN = 1024
PROJ_RM = 256
ATTN_TQ = 1024
ATTN_TK = 512
ATTN_RT = 128
ATTN_GUARD = 2.0 ** 64
ATTN_UNROLL = 5
OUT_TM = 256
RET_CHUNK = 256
RET_UNROLL = 4
NORM_TM = 512


def _params(sem, vmem_bytes):
    return pltpu.CompilerParams(
        dimension_semantics=sem, vmem_limit_bytes=min(int(vmem_bytes), V7X_VMEM_BUDGET)
    )


def _rmsnorm_kernel(x_ref, g_ref, o_ref):
    x = x_ref[...]
    ms = jnp.mean(x * x, axis=-1, keepdims=True)
    y = x * lax.rsqrt(ms + NORM_EPS)
    o_ref[...] = (y * g_ref[...]).astype(o_ref.dtype)


def _rmsnorm(x2d, gain, tm):
    t, d = x2d.shape
    tm = min(tm, t)
    return pl.pallas_call(
        _rmsnorm_kernel,
        grid=(t // tm,),
        in_specs=[pl.BlockSpec((tm, d), lambda i: (i, 0)), pl.BlockSpec((1, d), lambda i: (0, 0))],
        out_specs=pl.BlockSpec((tm, d), lambda i: (i, 0)),
        out_shape=jax.ShapeDtypeStruct((t, d), BF16),
        compiler_params=_params(("arbitrary",), 2 * tm * d * (4 + 2) + 4 * tm * d * 4),
        name="rmsnorm_in",
    )(x2d, gain.reshape(1, d))


def _proj_plain_kernel(u_ref, w_ref, o_ref, *, rm):
    for c in range(u_ref.shape[0] // rm):
        rows = pl.ds(c * rm, rm)
        o_ref[rows, :] = jnp.dot(u_ref[rows, :], w_ref[...], preferred_element_type=F32).astype(o_ref.dtype)


def _proj_plain_side_kernel(u_ref, w_ref, xs_ref, gs_ref, o_ref, us_ref, *, rm):
    _proj_plain_kernel(u_ref, w_ref, o_ref, rm=rm)
    _rmsnorm_kernel(xs_ref, gs_ref, us_ref)


def _proj_rot_kernel(u_ref, w_ref, cos_ref, sin1_ref, sin2_ref, mult_ref, o_ref, *, rm, shift1, shift2):
    for c in range(u_ref.shape[0] // rm):
        rows = pl.ds(c * rm, rm)
        acc = jnp.dot(u_ref[rows, :], w_ref[...], preferred_element_type=F32)
        cos, sin1, sin2, mult = cos_ref[rows, :], sin1_ref[rows, :], sin2_ref[rows, :], mult_ref[rows, :]
        for g in range(acc.shape[1] // LANES):
            x = acc[:, g * LANES:(g + 1) * LANES]
            y = x * cos + pltpu.roll(x, shift1, 1) * sin1 + pltpu.roll(x, shift2, 1) * sin2
            o_ref[rows, g * LANES:(g + 1) * LANES] = (y * mult).astype(o_ref.dtype)


def _proj_plain(u, w_bf16, tm, tn, rm, side=None):
    t, d = u.shape
    tm = min(tm, t)
    n_a = (RQ_OFF - AV_OFF) // tn
    n_cols = (RQ_OFF - AV_OFF) + (PROJ_OUT - RV_OFF)
    vmem = 2 * tm * d * 2 + 2 * d * tn * 2 + 2 * tm * tn * 2 + 2 * min(rm, tm) * tn * 4
    n_j = n_cols // tn
    in_specs = [pl.BlockSpec((tm, d), lambda i, j: (i, 0)),
                pl.BlockSpec((d, tn), lambda i, j: (0, jnp.where(j < n_a, AV_OFF // tn + j, RV_OFF // tn + j - n_a)))]
    out_specs = pl.BlockSpec((tm, tn), lambda i, j: (i, j))
    out_shape = jax.ShapeDtypeStruct((t, n_cols), BF16)
    args = (u, w_bf16)
    body = _proj_plain_kernel
    if side is not None:
        body = _proj_plain_side_kernel
        xs, gain = side
        rs = xs.shape[0] // ((t // tm) * n_j)
        side_spec = pl.BlockSpec((rs, d), lambda i, j: (i * n_j + j, 0))
        in_specs += [side_spec, pl.BlockSpec((1, d), lambda i, j: (0, 0))]
        out_specs = (out_specs, side_spec)
        out_shape = (out_shape, jax.ShapeDtypeStruct(xs.shape, BF16))
        args += (xs, gain.reshape(1, d))
        vmem += 2 * rs * d * (4 + 2) + 4 * rs * d * 4
    return pl.pallas_call(
        functools.partial(body, rm=min(rm, tm)),
        grid=(t // tm, n_j),
        in_specs=in_specs,
        out_specs=out_specs,
        out_shape=out_shape,
        compiler_params=_params(("arbitrary", "arbitrary"), vmem),
        name="in_proj_plain",
    )(*args)


def _proj_rot(u, w_bf16, tables, mults, col_off, n_cols, shifts, rows_per_seq, tm, tn, rm, name):
    t, d = u.shape
    tm = min(tm, rows_per_seq)
    n_seq_tiles = rows_per_seq // tm
    n_j = n_cols // tn
    tab_spec = pl.BlockSpec((tm, LANES), lambda i, j: (i % n_seq_tiles, 0))
    mult_spec = pl.BlockSpec((None, tm, LANES), lambda i, j: (j // (n_j // 2), i % n_seq_tiles, 0))
    vmem = 2 * tm * d * 2 + 2 * d * tn * 2 + 2 * tm * tn * 2 + 2 * 4 * tm * LANES * 4 + 3 * min(rm, tm) * tn * 4
    return pl.pallas_call(
        functools.partial(_proj_rot_kernel, rm=min(rm, tm), shift1=shifts[0], shift2=shifts[1]),
        grid=(t // tm, n_j),
        in_specs=[pl.BlockSpec((tm, d), lambda i, j: (i, 0)),
                  pl.BlockSpec((d, tn), lambda i, j: (0, col_off // tn + j)),
                  tab_spec, tab_spec, tab_spec, mult_spec],
        out_specs=pl.BlockSpec((tm, tn), lambda i, j: (i, j)),
        out_shape=jax.ShapeDtypeStruct((t, n_cols), BF16),
        compiler_params=_params(("arbitrary", "arbitrary"), vmem),
        name=name,
    )(u, w_bf16, *tables, mults)


def _side_rmsnorm_fits(t_main, t_side):
    steps = (t_main // min(PROJ_TM, t_main)) * ((RQ_OFF - AV_OFF + PROJ_OUT - RV_OFF) // PROJ_TN)
    return t_side % steps == 0 and (t_side // steps) % BF16_SUBLANES == 0


def _project(u, w_bf16, tabs, rows_per_seq, side=None):
    proj_a = _proj_rot(u, w_bf16, tabs["attn"], tabs["attn_mult"], AQ_OFF, AV_OFF - AQ_OFF,
                       (LANES - A_ROT_DIM // 2, A_ROT_DIM // 2), rows_per_seq, PROJ_TM, PROJ_TN, PROJ_RM, "in_proj_attn")
    proj_p = _proj_plain(u, w_bf16, PROJ_TM, PROJ_TN, PROJ_RM, side)
    u_side = None
    if side is not None:
        proj_p, u_side = proj_p
    proj_r = _proj_rot(u, w_bf16, tabs["ret"], tabs["ret_mult"], RQ_OFF, RV_OFF - RQ_OFF,
                       (R_QK_DIM // 2, R_QK_DIM // 2), rows_per_seq, PROJ_TM, PROJ_TN, PROJ_RM, "in_proj_ret")
    return (proj_a, proj_p, proj_r), u_side


def _rotary_tables(pos):
    n = pos.shape[0]
    posf = pos.astype(F32)[:, None]
    half = A_ROT_DIM // 2
    inv = jnp.power(jnp.float32(ROPE_THETA), -jnp.arange(half, dtype=F32) * 2.0 / A_ROT_DIM)
    ang = posf * inv[None, :]
    cos, sin = jnp.cos(ang), jnp.sin(ang)
    ones = jnp.ones((n, LANES - A_ROT_DIM), F32)
    zeros_h = jnp.zeros((n, half), F32)
    zeros_r = jnp.zeros((n, LANES - A_ROT_DIM), F32)
    cos_a = jnp.concatenate([cos, cos, ones], axis=1)
    sin_hi = jnp.concatenate([-sin, zeros_h, zeros_r], axis=1)
    sin_lo = jnp.concatenate([zeros_h, sin, zeros_r], axis=1)
    half_r = R_QK_DIM // 2
    inv_r = jnp.power(jnp.float32(RET_THETA), -jnp.arange(half_r, dtype=F32) * 2.0 / R_QK_DIM)
    ang_r = posf * inv_r[None, :]
    cos_r = jnp.concatenate([jnp.cos(ang_r), jnp.cos(ang_r)], axis=1)
    sin_r = jnp.concatenate([-jnp.sin(ang_r), jnp.sin(ang_r)], axis=1)
    one = jnp.ones((n, LANES), F32)
    kmask = jnp.broadcast_to((pos >= 0).astype(F32)[:, None], (n, LANES))
    return dict(
        attn=(cos_a, sin_hi, sin_lo),
        attn_mult=jnp.stack([one * ATTN_SCORE_SCALE, one]),
        ret=(cos_r, sin_r, jnp.zeros((n, LANES), F32)),
        ret_mult=jnp.stack([one, (R_QK_DIM ** -0.5) * kmask]),
    )


def _silu(g):
    h = 0.5 * g
    return h + h * jnp.tanh(h)


def _attn_kernel(q_ref, k_ref, v_ref, g_ref, mk_ref, mv_ref, mbias_ref, lq1_ref, lk1_ref, lq2_ref, lk2_ref,
                 gain_ref, o_ref, m_sc, l_sc, acc_sc, *, tk, rt):
    tq = q_ref.shape[0]
    n_kb = k_ref.shape[0] // tk
    nt_dims = (((1,), (1,)), ((), ()))
    sub = (pl.ds(0, A_QK_DIM), pl.ds(A_QK_DIM, A_QK_DIM))

    def lane_tiles(x):
        return [x[:, t * LANES:(t + 1) * LANES] for t in range(x.shape[1] // LANES)]

    def update(idx, rows, s, v_blk, first):
        tiles = lane_tiles(s)
        mx = functools.reduce(jnp.maximum, tiles)
        m_cur = jnp.max(mx, axis=-1, keepdims=True)
        if first:
            m_new = jnp.broadcast_to(m_cur, (rt, LANES))
        else:
            m_prev = m_sc[idx, rows, :]
            m_new = jnp.maximum(m_prev, m_cur)
        p_tiles = [jnp.exp2(t - m_new) for t in tiles]
        p = jnp.concatenate(p_tiles, axis=1) if len(p_tiles) > 1 else p_tiles[0]
        pv = jnp.dot(p.astype(BF16), v_blk, preferred_element_type=F32)
        l_part = functools.reduce(jnp.add, p_tiles)
        if first:
            l_sc[idx, rows, :] = l_part
            acc_sc[idx, rows, :] = pv
        else:
            alpha = jnp.exp2(m_prev - m_new)
            l_sc[idx, rows, :] = alpha * l_sc[idx, rows, :] + l_part
            acc_sc[idx, rows, :] = jnp.concatenate([alpha, alpha], axis=1) * acc_sc[idx, rows, :] + pv
        m_sc[idx, rows, :] = m_new

    def scores(rows, idx, keys):
        return lax.dot_general(q_ref[rows, sub[idx]], keys, nt_dims, preferred_element_type=F32)

    def update_fixed(idx, rows, s, v_blk):
        tiles = lane_tiles(s)
        m_ref = m_sc[idx, rows, :]
        p_tiles = [jnp.exp2(t - m_ref) for t in tiles]
        p = jnp.concatenate(p_tiles, axis=1)
        acc_sc[idx, rows, :] += jnp.dot(p.astype(BF16), v_blk, preferred_element_type=F32)
        l_sc[idx, rows, :] += functools.reduce(jnp.add, p_tiles)

    def first_step():
        mbias = mbias_ref[...] * LOG2E
        v_first = jnp.concatenate([mv_ref[...], v_ref[pl.ds(0, tk), :]], axis=0)
        for r in range(tq // rt):
            rows = pl.ds(r * rt, rt)
            for idx in range(2):
                s_meta = scores(rows, idx, mk_ref[:, sub[idx]]) + mbias
                s_tok = scores(rows, idx, k_ref[pl.ds(0, tk), sub[idx]])
                update(idx, rows, jnp.concatenate([s_meta, s_tok], axis=1), v_first, True)

    def kv_loop(step):
        def body(kb, carry):
            off = pl.multiple_of(kb * tk, tk)
            v_blk = v_ref[pl.ds(off, tk), :]
            for r in range(tq // rt):
                rows = pl.ds(r * rt, rt)
                for idx in range(2):
                    step(idx, rows, scores(rows, idx, k_ref[pl.ds(off, tk), sub[idx]]), v_blk)
            return carry

        lax.fori_loop(1, n_kb, body, 0, unroll=ATTN_UNROLL)

    def finish():
        lam = (jnp.exp(jnp.sum(lq1_ref[...] * lk1_ref[...], axis=-1, keepdims=True))
               - jnp.exp(jnp.sum(lq2_ref[...] * lk2_ref[...], axis=-1, keepdims=True)) + LAMBDA_INIT)
        inv1 = 1.0 / jnp.sum(l_sc[0], axis=-1, keepdims=True)
        inv2 = lam / jnp.sum(l_sc[1], axis=-1, keepdims=True)
        o = acc_sc[0] * inv1 - acc_sc[1] * inv2
        ms = jnp.mean(o * o, axis=-1, keepdims=True)
        y = (o * lax.rsqrt(ms + NORM_EPS)) * (gain_ref[...] * (1.0 - LAMBDA_INIT))
        o_ref[...] = (y * _silu(g_ref[...].astype(F32))).astype(o_ref.dtype)
        return ms

    first_step()
    kv_loop(update_fixed)
    ms = finish()
    in_range = jnp.minimum(jnp.min(jnp.where(l_sc[...] < ATTN_GUARD, 1.0, 0.0)),
                           jnp.min(jnp.where(ms < F32_MAX, 1.0, 0.0)))

    @pl.when(in_range < 0.5)
    def _():
        first_step()
        kv_loop(lambda idx, rows, s, v_blk: update(idx, rows, s, v_blk, False))
        finish()


def _attention(projs, metas, mbias, lams, subln_gain, batch, seq, tq, tk):
    t = batch * seq
    tq, tk = min(tq, seq), min(tk, seq)
    nq = seq // tq
    hb = A_V_DIM
    proj_a, proj_p, _ = projs
    meta_a, meta_p, _ = metas
    q_spec = pl.BlockSpec((tq, hb), lambda b, h, i: (b * nq + i, h))
    k_spec = pl.BlockSpec((seq, hb), lambda b, h, i: (b, A_HEADS + h))
    v_spec = pl.BlockSpec((seq, hb), lambda b, h, i: (b, h))
    g_spec = pl.BlockSpec((tq, hb), lambda b, h, i: (b * nq + i, A_HEADS + h))
    mk_spec = pl.BlockSpec((CHUNK, hb), lambda b, h, i: (0, A_HEADS + h))
    mv_spec = pl.BlockSpec((CHUNK, hb), lambda b, h, i: (0, h))
    row128 = pl.BlockSpec((1, LANES), lambda b, h, i: (0, 0))
    gain_spec = pl.BlockSpec((1, A_V_DIM), lambda b, h, i: (0, 0))
    vmem = (2 * 2 * seq * hb * 2 + 2 * 3 * tq * hb * 2 + 2 * 2 * CHUNK * hb * 2
            + 2 * tq * (2 * LANES + A_V_DIM) * 4 + 6 * tq * tk * 4)
    return pl.pallas_call(
        functools.partial(_attn_kernel, tk=tk, rt=min(ATTN_RT, tq)),
        grid=(batch, A_HEADS, nq),
        in_specs=[q_spec, k_spec, v_spec, g_spec, mk_spec, mv_spec, row128, row128, row128, row128, row128,
                  gain_spec],
        out_specs=pl.BlockSpec((tq, hb), lambda b, h, i: (b * nq + i, h)),
        out_shape=jax.ShapeDtypeStruct((t, A_HEADS * A_V_DIM), BF16),
        scratch_shapes=[pltpu.VMEM((2, tq, LANES), F32), pltpu.VMEM((2, tq, LANES), F32),
                        pltpu.VMEM((2, tq, A_V_DIM), F32)],
        compiler_params=_params(("arbitrary", "arbitrary", "arbitrary"), vmem),
        name="diff_attention",
    )(proj_a, proj_a, proj_p, proj_p, meta_a, meta_p, mbias, *lams, subln_gain.reshape(1, A_V_DIM))


def _log_sigmoid(x):
    return jnp.minimum(x, 0.0) - jnp.log(1.0 + jnp.exp(-jnp.abs(x)))


def _ret_kernel(q_ref, k_ref, v_ref, g_ref, mk_ref, mv_ref, decf_ref, decb_ref, gain_ref, o_ref, u_sc, st_sc):
    c = RET_CHUNK
    dk, dv = R_QK_DIM, R_V_DIM
    n_chunks = q_ref.shape[0] // c
    nt_dims = (((1,), (1,)), ((), ()))
    lg_f = _log_sigmoid(decf_ref[0])
    lg_b = _log_sigmoid(decb_ref[0])
    lg_f1, lg_b1 = lg_f[:, :1], lg_b[:, :1]

    row = lax.broadcasted_iota(jnp.int32, (c, c), 0)
    col = lax.broadcasted_iota(jnp.int32, (c, c), 1)
    rel = (row - col).astype(F32)
    fwd_mask = rel >= 0
    d_both = jnp.where(fwd_mask, jnp.exp(jnp.where(fwd_mask, rel, 0.0) * lg_f1),
                       jnp.exp(jnp.where(fwd_mask, 0.0, -rel) * lg_b1))
    r_v = lax.broadcasted_iota(jnp.int32, (c, dv), 0).astype(F32)
    r_k = lax.broadcasted_iota(jnp.int32, (c, dk), 0).astype(F32)
    qd = jnp.concatenate([jnp.exp((r_v + 1.0) * lg_f), jnp.exp((c - r_v) * lg_b)], axis=1)
    kd = jnp.concatenate([jnp.exp((c - 1.0 - r_k) * lg_f1), jnp.exp(r_k * lg_b1)], axis=1)
    cd_f = jnp.exp(c * lg_f)
    cd_b = jnp.exp(c * lg_b)

    def chunk_rows(i):
        return pl.ds(pl.multiple_of(i * c, c), c)

    def loop_a(i, carry):
        rows = chunk_rows(i)
        k_blk = k_ref[rows, :].astype(F32)
        kdec = (jnp.concatenate([k_blk, k_blk], axis=1) * kd).T.astype(BF16)
        u_sc[i] = jnp.dot(kdec, v_ref[rows, :], preferred_element_type=F32)
        return carry

    lax.fori_loop(0, n_chunks, loop_a, 0, unroll=RET_UNROLL)

    r_m = lax.broadcasted_iota(jnp.int32, (CHUNK, dk), 0).astype(F32)
    mk_dec = (mk_ref[...].astype(F32) * jnp.exp((CHUNK - 1.0 - r_m) * lg_f1)).T.astype(BF16)
    state0 = jnp.dot(mk_dec, mv_ref[...], preferred_element_type=F32)

    def loop_b(i, carry):
        st_f, st_b = carry
        j = n_chunks - 1 - i
        st_sc[i, :, pl.ds(0, dv)] = st_f.astype(BF16)
        st_sc[j, :, pl.ds(dv, dv)] = st_b.astype(BF16)
        return st_f * cd_f + u_sc[i, pl.ds(0, dk), :], st_b * cd_b + u_sc[j, pl.ds(dk, dk), :]

    lax.fori_loop(0, n_chunks, loop_b, (state0, jnp.zeros((dk, dv), F32)), unroll=RET_UNROLL)

    gain = gain_ref[0]

    def loop_c(i, carry):
        rows = chunk_rows(i)
        q_blk, v_blk = q_ref[rows, :], v_ref[rows, :]
        s = lax.dot_general(q_blk, k_ref[rows, :], nt_dims, preferred_element_type=F32) * d_both
        intra = jnp.dot(s.astype(BF16), v_blk, preferred_element_type=F32)
        inter = jnp.dot(q_blk, st_sc[i], preferred_element_type=F32) * qd
        ro = intra + (inter[:, :dv] + inter[:, dv:])
        ms = jnp.mean(ro * ro, axis=-1, keepdims=True)
        y = (ro * lax.rsqrt(ms + NORM_EPS)) * gain
        o_ref[rows, :] = (y * _silu(g_ref[rows, :].astype(F32))).astype(o_ref.dtype)
        return carry

    lax.fori_loop(0, n_chunks, loop_c, 0, unroll=RET_UNROLL)


def _retention(projs, metas, dec_f, dec_b, ret_gain, batch, seq):
    t = batch * seq
    kb, vb = R_QK_DIM, R_V_DIM
    _, proj_p, proj_r = projs
    _, meta_p, meta_r = metas
    q_spec = pl.BlockSpec((seq, kb), lambda b, h: (b, h))
    k_spec = pl.BlockSpec((seq, kb), lambda b, h: (b, R_HEADS + h))
    v_spec = pl.BlockSpec((seq, vb), lambda b, h: (b, 2 * A_HEADS + h))
    g_spec = pl.BlockSpec((seq, vb), lambda b, h: (b, 2 * A_HEADS + R_HEADS + h))
    mk_spec = pl.BlockSpec((CHUNK, kb), lambda b, h: (0, R_HEADS + h))
    mv_spec = pl.BlockSpec((CHUNK, vb), lambda b, h: (0, 2 * A_HEADS + h))
    head_spec = pl.BlockSpec((1, 1, vb), lambda b, h: (h, 0, 0))
    n_chunks = seq // RET_CHUNK
    vmem = 2 * seq * (2 * kb + 3 * vb) * 2 + n_chunks * 2 * kb * vb * (4 + 2) + 64 * RET_CHUNK * vb * 4
    return pl.pallas_call(
        _ret_kernel,
        grid=(batch, R_HEADS),
        in_specs=[q_spec, k_spec, v_spec, g_spec, mk_spec, mv_spec, head_spec, head_spec, head_spec],
        out_specs=pl.BlockSpec((seq, vb), lambda b, h: (b, h)),
        out_shape=jax.ShapeDtypeStruct((t, R_HEADS * R_V_DIM), BF16),
        scratch_shapes=[pltpu.VMEM((n_chunks, 2 * kb, vb), F32), pltpu.VMEM((n_chunks, kb, 2 * vb), BF16)],
        compiler_params=_params(("arbitrary", "arbitrary"), vmem),
        name="retention",
    )(proj_r, proj_r, proj_p, proj_p, meta_r, meta_p, dec_f, dec_b, ret_gain)


def _out_kernel(a_ref, r_ref, w_ref, x_ref, g_ref, o_ref):
    half = a_ref.shape[1]
    h = x_ref[...] + jnp.dot(a_ref[...], w_ref[pl.ds(0, half), :], preferred_element_type=F32)
    h = h + jnp.dot(r_ref[...], w_ref[pl.ds(half, half), :], preferred_element_type=F32)
    ms = jnp.mean(h * h, axis=-1, keepdims=True)
    o_ref[...] = (h * lax.rsqrt(ms + NORM_EPS)) * g_ref[...]


def _out_proj(ao, ro, w_bf16, x2d, gain, tm):
    t, d = x2d.shape
    tm = min(tm, t)
    half = d // 2
    vmem = d * d * 2 + 2 * 2 * tm * half * 2 + 2 * 2 * tm * d * 4 + 2 * tm * d * 4
    return pl.pallas_call(
        _out_kernel,
        grid=(t // tm,),
        in_specs=[
            pl.BlockSpec((tm, half), lambda i: (i, 0)),
            pl.BlockSpec((tm, half), lambda i: (i, 0)),
            pl.BlockSpec((d, d), lambda i: (0, 0), pipeline_mode=pl.Buffered(1)),
            pl.BlockSpec((tm, d), lambda i: (i, 0)),
            pl.BlockSpec((1, d), lambda i: (0, 0)),
        ],
        out_specs=pl.BlockSpec((tm, d), lambda i: (i, 0)),
        out_shape=jax.ShapeDtypeStruct((t, d), F32),
        compiler_params=_params(("arbitrary",), vmem),
        name="out_proj",
    )(ao, ro, w_bf16, x2d, gain.reshape(1, d))


def _trunk(x, shared, u=None, side_x=None):
    batch, seq, d = x.shape
    x2d = x.reshape(batch * seq, d)
    if u is None:
        u = _rmsnorm(x2d, shared["norm_gain"], NORM_TM)
    side = None if side_x is None else (side_x, shared["norm_gain"])
    projs, u_side = _project(u, shared["w_in"], _rotary_tables(jnp.arange(seq, dtype=jnp.int32) + N_META), seq, side)
    ao = _attention(projs, shared["proj_meta"], shared["mbias"], shared["lams"], shared["subln_gain"],
                    batch, seq, ATTN_TQ, ATTN_TK)
    ro = _retention(projs, shared["proj_meta"], shared["dec_f"], shared["dec_b"], shared["ret_gain"], batch, seq)
    y = _out_proj(ao, ro, shared["w_out"], x2d, shared["final_gain"], OUT_TM)
    return y.reshape(batch, seq, d), u_side


def kernel(x_prompt, x_sample, meta_tokens, norm_gain, w_in, lambda_q1, lambda_k1, lambda_q2, lambda_k2,
           attn_subln_gain, ret_decay_fwd, ret_decay_bwd, ret_norm_gain, w_out, final_norm_gain):
    assert norm_gain.shape[0] == 1, "single-layer trunk"
    d = x_prompt.shape[-1]
    assert d == D_MODEL and w_in.shape[1:] == (D_MODEL, PROJ_OUT) and w_out.shape[1:] == (D_MODEL, D_MODEL)
    w_in_bf16 = w_in[0].astype(BF16)
    w_out_bf16 = w_out[0].astype(BF16)

    meta_pos = jnp.arange(CHUNK, dtype=jnp.int32) - (CHUNK - N_META)
    meta_rows = jnp.concatenate([jnp.zeros((CHUNK - N_META, d), F32), meta_tokens.astype(F32)], axis=0)
    u_meta = _rmsnorm(meta_rows, norm_gain[0], CHUNK)
    proj_meta, _ = _project(u_meta, w_in_bf16, _rotary_tables(meta_pos), CHUNK)
    mbias = jnp.where(meta_pos >= 0, 0.0, NEG_BIG).astype(F32).reshape(1, CHUNK)

    def per_head(v):
        return jnp.broadcast_to(v.astype(F32)[:, None, None], (R_HEADS, 1, R_V_DIM))

    shared = dict(
        norm_gain=norm_gain[0], w_in=w_in_bf16, w_out=w_out_bf16, proj_meta=proj_meta, mbias=mbias,
        lams=tuple(v[0].astype(F32).reshape(1, A_QK_DIM) for v in (lambda_q1, lambda_k1, lambda_q2, lambda_k2)),
        subln_gain=attn_subln_gain[0].astype(F32),
        dec_f=per_head(ret_decay_fwd[0]), dec_b=per_head(ret_decay_bwd[0]),
        ret_gain=ret_norm_gain[0].astype(F32).reshape(R_HEADS, 1, R_V_DIM),
        final_gain=final_norm_gain.astype(F32),
    )
    xs2d = x_sample.reshape(-1, d)
    side_x = xs2d if _side_rmsnorm_fits(x_prompt.shape[0] * x_prompt.shape[1], xs2d.shape[0]) else None
    y_prompt, u_sample = _trunk(x_prompt, shared, side_x=side_x)
    y_sample, _ = _trunk(x_sample, shared, u=u_sample)
    return (y_prompt, y_sample)
```

```python
import functools
import math

import jax
import jax.numpy as jnp
from jax import lax
from jax.experimental import pallas as pl
from jax.experimental.pallas import tpu as pltpu

F32 = jnp.float32
BF16 = jnp.bfloat16

D_MODEL = 4096
N_META = 16
CHUNK = 128
A_HEADS = 8
A_QK_DIM = 128
A_V_DIM = 256
A_ROT_DIM = A_QK_DIM // 4
ROPE_THETA = 500000.0
R_HEADS = 8
R_QK_DIM = 128
R_V_DIM = 256
RET_THETA = 10000.0
NORM_EPS = 1e-6
NEG_BIG = -1e30
LAMBDA_INIT = 0.8 - 0.6 * math.exp(-0.3 * 0)

AQ_OFF = 0
AV_OFF = AQ_OFF + 2 * (2 * A_HEADS * A_QK_DIM)
RQ_OFF = AV_OFF + 2 * (A_HEADS * A_V_DIM)
RV_OFF = RQ_OFF + 2 * (R_HEADS * R_QK_DIM)
PROJ_OUT = RV_OFF + 2 * (R_HEADS * R_V_DIM)

LANES = 128
BF16_SUBLANES = 16
LOG2E = math.log2(math.e)
F32_MAX = float(jnp.finfo(jnp.float32).max)
ATTN_SCORE_SCALE = (A_QK_DIM ** -0.5) * LOG2E
V7X_VMEM_BYTES = 64 * 1024 * 1024
V7X_VMEM_BUDGET = V7X_VMEM_BYTES - 4 * 1024 * 1024

PROJ_TM = 1024
PROJ_TN = 1024
PROJ_RM = 256
ATTN_TQ = 1024
ATTN_TK = 512
ATTN_RT = 128
ATTN_GUARD = 2.0 ** 64
ATTN_UNROLL = 5
ATTN_FAST_UNROLL = 15
OUT_TM = 256
RET_CHUNK = 256
RET_UNROLL = 4
NORM_TM = 512


def _params(sem, vmem_bytes):
    return pltpu.CompilerParams(
        dimension_semantics=sem, vmem_limit_bytes=min(int(vmem_bytes), V7X_VMEM_BUDGET)
    )


def _rmsnorm_kernel(x_ref, g_ref, o_ref):
    x = x_ref[...]
    ms = jnp.mean(x * x, axis=-1, keepdims=True)
    y = x * lax.rsqrt(ms + NORM_EPS)
    o_ref[...] = (y * g_ref[...]).astype(o_ref.dtype)


def _rmsnorm(x2d, gain, tm):
    t, d = x2d.shape
    tm = min(tm, t)
    return pl.pallas_call(
        _rmsnorm_kernel,
        grid=(t // tm,),
        in_specs=[pl.BlockSpec((tm, d), lambda i: (i, 0)), pl.BlockSpec((1, d), lambda i: (0, 0))],
        out_specs=pl.BlockSpec((tm, d), lambda i: (i, 0)),
        out_shape=jax.ShapeDtypeStruct((t, d), BF16),
        compiler_params=_params(("arbitrary",), 2 * tm * d * (4 + 2) + 4 * tm * d * 4),
        name="rmsnorm_in",
    )(x2d, gain.reshape(1, d))


def _proj_plain_kernel(u_ref, w_ref, o_ref, *, rm):
    for c in range(u_ref.shape[0] // rm):
        rows = pl.ds(c * rm, rm)
        o_ref[rows, :] = jnp.dot(u_ref[rows, :], w_ref[...], preferred_element_type=F32).astype(o_ref.dtype)


def _proj_plain_side_kernel(u_ref, w_ref, xs_ref, gs_ref, o_ref, us_ref, *, rm):
    _proj_plain_kernel(u_ref, w_ref, o_ref, rm=rm)
    _rmsnorm_kernel(xs_ref, gs_ref, us_ref)


def _proj_rot_kernel(u_ref, w_ref, cos_ref, sin1_ref, sin2_ref, mult_ref, o_ref, *, rm, shift1, shift2):
    for c in range(u_ref.shape[0] // rm):
        rows = pl.ds(c * rm, rm)
        acc = jnp.dot(u_ref[rows, :], w_ref[...], preferred_element_type=F32)
        cos, sin1, sin2, mult = cos_ref[rows, :], sin1_ref[rows, :], sin2_ref[rows, :], mult_ref[rows, :]
        for g in range(acc.shape[1] // LANES):
            x = acc[:, g * LANES:(g + 1) * LANES]
            y = x * cos + pltpu.roll(x, shift1, 1) * sin1 + pltpu.roll(x, shift2, 1) * sin2
            o_ref[rows, g * LANES:(g + 1) * LANES] = (y * mult).astype(o_ref.dtype)


def _proj_plain(u, w_bf16, tm, tn, rm, side=None):
    t, d = u.shape
    tm = min(tm, t)
    n_a = (RQ_OFF - AV_OFF) // tn
    n_cols = (RQ_OFF - AV_OFF) + (PROJ_OUT - RV_OFF)
    vmem = 2 * tm * d * 2 + 2 * d * tn * 2 + 2 * tm * tn * 2 + 2 * min(rm, tm) * tn * 4
    n_j = n_cols // tn
    in_specs = [pl.BlockSpec((tm, d), lambda i, j: (i, 0)),
                pl.BlockSpec((d, tn), lambda i, j: (0, jnp.where(j < n_a, AV_OFF // tn + j, RV_OFF // tn + j - n_a)))]
    out_specs = pl.BlockSpec((tm, tn), lambda i, j: (i, j))
    out_shape = jax.ShapeDtypeStruct((t, n_cols), BF16)
    args = (u, w_bf16)
    body = _proj_plain_kernel
    if side is not None:
        body = _proj_plain_side_kernel
        xs, gain = side
        rs = xs.shape[0] // ((t // tm) * n_j)
        side_spec = pl.BlockSpec((rs, d), lambda i, j: (i * n_j + j, 0))
        in_specs += [side_spec, pl.BlockSpec((1, d), lambda i, j: (0, 0))]
        out_specs = (out_specs, side_spec)
        out_shape = (out_shape, jax.ShapeDtypeStruct(xs.shape, BF16))
        args += (xs, gain.reshape(1, d))
        vmem += 2 * rs * d * (4 + 2) + 4 * rs * d * 4
    return pl.pallas_call(
        functools.partial(body, rm=min(rm, tm)),
        grid=(t // tm, n_j),
        in_specs=in_specs,
        out_specs=out_specs,
        out_shape=out_shape,
        compiler_params=_params(("arbitrary", "arbitrary"), vmem),
        name="in_proj_plain",
    )(*args)


def _proj_rot(u, w_bf16, tables, mults, col_off, n_cols, shifts, rows_per_seq, tm, tn, rm, name):
    t, d = u.shape
    tm = min(tm, rows_per_seq)
    n_seq_tiles = rows_per_seq // tm
    n_j = n_cols // tn
    tab_spec = pl.BlockSpec((tm, LANES), lambda i, j: (i % n_seq_tiles, 0))
    mult_spec = pl.BlockSpec((None, tm, LANES), lambda i, j: (j // (n_j // 2), i % n_seq_tiles, 0))
    vmem = 2 * tm * d * 2 + 2 * d * tn * 2 + 2 * tm * tn * 2 + 2 * 4 * tm * LANES * 4 + 3 * min(rm, tm) * tn * 4
    return pl.pallas_call(
        functools.partial(_proj_rot_kernel, rm=min(rm, tm), shift1=shifts[0], shift2=shifts[1]),
        grid=(t // tm, n_j),
        in_specs=[pl.BlockSpec((tm, d), lambda i, j: (i, 0)),
                  pl.BlockSpec((d, tn), lambda i, j: (0, col_off // tn + j)),
                  tab_spec, tab_spec, tab_spec, mult_spec],
        out_specs=pl.BlockSpec((tm, tn), lambda i, j: (i, j)),
        out_shape=jax.ShapeDtypeStruct((t, n_cols), BF16),
        compiler_params=_params(("arbitrary", "arbitrary"), vmem),
        name=name,
    )(u, w_bf16, *tables, mults)


def _side_rmsnorm_fits(t_main, t_side):
    steps = (t_main // min(PROJ_TM, t_main)) * ((RQ_OFF - AV_OFF + PROJ_OUT - RV_OFF) // PROJ_TN)
    return t_side % steps == 0 and (t_side // steps) % BF16_SUBLANES == 0


def _project(u, w_bf16, tabs, rows_per_seq, side=None):
    proj_a = _proj_rot(u, w_bf16, tabs["attn"], tabs["attn_mult"], AQ_OFF, AV_OFF - AQ_OFF,
                       (LANES - A_ROT_DIM // 2, A_ROT_DIM // 2), rows_per_seq, PROJ_TM, PROJ_TN, PROJ_RM, "in_proj_attn")
    proj_p = _proj_plain(u, w_bf16, PROJ_TM, PROJ_TN, PROJ_RM, side)
    u_side = None
    if side is not None:
        proj_p, u_side = proj_p
    proj_r = _proj_rot(u, w_bf16, tabs["ret"], tabs["ret_mult"], RQ_OFF, RV_OFF - RQ_OFF,
                       (R_QK_DIM // 2, R_QK_DIM // 2), rows_per_seq, PROJ_TM, PROJ_TN, PROJ_RM, "in_proj_ret")
    return (proj_a, proj_p, proj_r), u_side


def _rotary_tables(pos):
    n = pos.shape[0]
    posf = pos.astype(F32)[:, None]
    half = A_ROT_DIM // 2
    inv = jnp.power(jnp.float32(ROPE_THETA), -jnp.arange(half, dtype=F32) * 2.0 / A_ROT_DIM)
    ang = posf * inv[None, :]
    cos, sin = jnp.cos(ang), jnp.sin(ang)
    ones = jnp.ones((n, LANES - A_ROT_DIM), F32)
    zeros_h = jnp.zeros((n, half), F32)
    zeros_r = jnp.zeros((n, LANES - A_ROT_DIM), F32)
    cos_a = jnp.concatenate([cos, cos, ones], axis=1)
    sin_hi = jnp.concatenate([-sin, zeros_h, zeros_r], axis=1)
    sin_lo = jnp.concatenate([zeros_h, sin, zeros_r], axis=1)
    half_r = R_QK_DIM // 2
    inv_r = jnp.power(jnp.float32(RET_THETA), -jnp.arange(half_r, dtype=F32) * 2.0 / R_QK_DIM)
    ang_r = posf * inv_r[None, :]
    cos_r = jnp.concatenate([jnp.cos(ang_r), jnp.cos(ang_r)], axis=1)
    sin_r = jnp.concatenate([-jnp.sin(ang_r), jnp.sin(ang_r)], axis=1)
    one = jnp.ones((n, LANES), F32)
    kmask = jnp.broadcast_to((pos >= 0).astype(F32)[:, None], (n, LANES))
    return dict(
        attn=(cos_a, sin_hi, sin_lo),
        attn_mult=jnp.stack([one * ATTN_SCORE_SCALE, one]),
        ret=(cos_r, sin_r, jnp.zeros((n, LANES), F32)),
        ret_mult=jnp.stack([one, (R_QK_DIM ** -0.5) * kmask]),
    )


def _silu(g):
    h = 0.5 * g
    return h + h * jnp.tanh(h)


def _attn_kernel(q_ref, k_ref, v_ref, g_ref, mk_ref, mv_ref, mbias_ref, lq1_ref, lk1_ref, lq2_ref, lk2_ref,
                 gain_ref, o_ref, m_sc, l_sc, acc_sc, *, tk, rt):
    tq = q_ref.shape[0]
    n_kb = k_ref.shape[0] // tk
    nt_dims = (((1,), (1,)), ((), ()))
    sub = (pl.ds(0, A_QK_DIM), pl.ds(A_QK_DIM, A_QK_DIM))

    def lane_tiles(x):
        return [x[:, t * LANES:(t + 1) * LANES] for t in range(x.shape[1] // LANES)]

    def update(idx, rows, s, v_blk, first):
        tiles = lane_tiles(s)
        mx = functools.reduce(jnp.maximum, tiles)
        m_cur = jnp.max(mx, axis=-1, keepdims=True)
        if first:
            m_new = jnp.broadcast_to(m_cur, (rt, LANES))
        else:
            m_prev = m_sc[idx, rows, :]
            m_new = jnp.maximum(m_prev, m_cur)
        p_tiles = [jnp.exp2(t - m_new) for t in tiles]
        p = jnp.concatenate(p_tiles, axis=1) if len(p_tiles) > 1 else p_tiles[0]
        pv = jnp.dot(p.astype(BF16), v_blk, preferred_element_type=F32)
        l_part = functools.reduce(jnp.add, p_tiles)
        if first:
            l_sc[idx, rows, :] = l_part
            acc_sc[idx, rows, :] = pv
        else:
            alpha = jnp.exp2(m_prev - m_new)
            l_sc[idx, rows, :] = alpha * l_sc[idx, rows, :] + l_part
            acc_sc[idx, rows, :] = jnp.concatenate([alpha, alpha], axis=1) * acc_sc[idx, rows, :] + pv
        m_sc[idx, rows, :] = m_new

    def scores(rows, idx, keys):
        return lax.dot_general(q_ref[rows, sub[idx]], keys, nt_dims, preferred_element_type=F32)

    def update_fixed(idx, rows, s, v_blk):
        tiles = lane_tiles(s)
        m_ref = m_sc[idx, rows, :]
        p_tiles = [jnp.exp2(t - m_ref) for t in tiles]
        p = jnp.concatenate(p_tiles, axis=1)
        acc_sc[idx, rows, :] += jnp.dot(p.astype(BF16), v_blk, preferred_element_type=F32)
        l_sc[idx, rows, :] += functools.reduce(jnp.add, p_tiles)

    def first_step():
        mbias = mbias_ref[...] * LOG2E
        v_first = jnp.concatenate([mv_ref[...], v_ref[pl.ds(0, tk), :]], axis=0)
        for r in range(tq // rt):
            rows = pl.ds(r * rt, rt)
            for idx in range(2):
                s_meta = scores(rows, idx, mk_ref[:, sub[idx]]) + mbias
                s_tok = scores(rows, idx, k_ref[pl.ds(0, tk), sub[idx]])
                update(idx, rows, jnp.concatenate([s_meta, s_tok], axis=1), v_first, True)

    def kv_loop(step, unroll):
        def body(kb, carry):
            off = pl.multiple_of(kb * tk, tk)
            v_blk = v_ref[pl.ds(off, tk), :]
            for r in range(tq // rt):
                rows = pl.ds(r * rt, rt)
                for idx in range(2):
                    step(idx, rows, scores(rows, idx, k_ref[pl.ds(off, tk), sub[idx]]), v_blk)
            return carry

        lax.fori_loop(1, n_kb, body, 0, unroll=unroll)

    def finish():
        lam = (jnp.exp(jnp.sum(lq1_ref[...] * lk1_ref[...], axis=-1, keepdims=True))
               - jnp.exp(jnp.sum(lq2_ref[...] * lk2_ref[...], axis=-1, keepdims=True)) + LAMBDA_INIT)
        inv1 = 1.0 / jnp.sum(l_sc[0], axis=-1, keepdims=True)
        inv2 = lam / jnp.sum(l_sc[1], axis=-1, keepdims=True)
        o = acc_sc[0] * inv1 - acc_sc[1] * inv2
        ms = jnp.mean(o * o, axis=-1, keepdims=True)
        y = (o * lax.rsqrt(ms + NORM_EPS)) * (gain_ref[...] * (1.0 - LAMBDA_INIT))
        o_ref[...] = (y * _silu(g_ref[...].astype(F32))).astype(o_ref.dtype)
        return ms

    first_step()
    kv_loop(update_fixed, ATTN_FAST_UNROLL)
    ms = finish()
    in_range = jnp.minimum(jnp.min(jnp.where(l_sc[...] < ATTN_GUARD, 1.0, 0.0)),
                           jnp.min(jnp.where(ms < F32_MAX, 1.0, 0.0)))

    @pl.when(in_range < 0.5)
    def _():
        first_step()
        kv_loop(functools.partial(update, first=False), ATTN_UNROLL)
        finish()


def _attention(projs, metas, mbias, lams, subln_gain, batch, seq, tq, tk):
    t = batch * seq
    tq, tk = min(tq, seq), min(tk, seq)
    nq = seq // tq
    hb = A_V_DIM
    proj_a, proj_p, _ = projs
    meta_a, meta_p, _ = metas
    q_spec = pl.BlockSpec((tq, hb), lambda b, h, i: (b * nq + i, h))
    k_spec = pl.BlockSpec((seq, hb), lambda b, h, i: (b, A_HEADS + h))
    v_spec = pl.BlockSpec((seq, hb), lambda b, h, i: (b, h))
    g_spec = pl.BlockSpec((tq, hb), lambda b, h, i: (b * nq + i, A_HEADS + h))
    mk_spec = pl.BlockSpec((CHUNK, hb), lambda b, h, i: (0, A_HEADS + h))
    mv_spec = pl.BlockSpec((CHUNK, hb), lambda b, h, i: (0, h))
    row128 = pl.BlockSpec((1, LANES), lambda b, h, i: (0, 0))
    gain_spec = pl.BlockSpec((1, A_V_DIM), lambda b, h, i: (0, 0))
    vmem = (2 * 2 * seq * hb * 2 + 2 * 3 * tq * hb * 2 + 2 * 2 * CHUNK * hb * 2
            + 2 * tq * (2 * LANES + A_V_DIM) * 4 + 6 * tq * tk * 4)
    return pl.pallas_call(
        functools.partial(_attn_kernel, tk=tk, rt=min(ATTN_RT, tq)),
        grid=(batch, A_HEADS, nq),
        in_specs=[q_spec, k_spec, v_spec, g_spec, mk_spec, mv_spec, row128, row128, row128, row128, row128,
                  gain_spec],
        out_specs=pl.BlockSpec((tq, hb), lambda b, h, i: (b * nq + i, h)),
        out_shape=jax.ShapeDtypeStruct((t, A_HEADS * A_V_DIM), BF16),
        scratch_shapes=[pltpu.VMEM((2, tq, LANES), F32), pltpu.VMEM((2, tq, LANES), F32),
                        pltpu.VMEM((2, tq, A_V_DIM), F32)],
        compiler_params=_params(("arbitrary", "arbitrary", "arbitrary"), vmem),
        name="diff_attention",
    )(proj_a, proj_a, proj_p, proj_p, meta_a, meta_p, mbias, *lams, subln_gain.reshape(1, A_V_DIM))


def _log_sigmoid(x):
    return jnp.minimum(x, 0.0) - jnp.log(1.0 + jnp.exp(-jnp.abs(x)))


def _ret_kernel(q_ref, k_ref, v_ref, g_ref, mk_ref, mv_ref, decf_ref, decb_ref, gain_ref, o_ref, u_sc, st_sc):
    c = RET_CHUNK
    dk, dv = R_QK_DIM, R_V_DIM
    n_chunks = q_ref.shape[0] // c
    nt_dims = (((1,), (1,)), ((), ()))
    lg_f = _log_sigmoid(decf_ref[0])
    lg_b = _log_sigmoid(decb_ref[0])
    lg_f1, lg_b1 = lg_f[:, :1], lg_b[:, :1]

    row = lax.broadcasted_iota(jnp.int32, (c, c), 0)
    col = lax.broadcasted_iota(jnp.int32, (c, c), 1)
    rel = (row - col).astype(F32)
    fwd_mask = rel >= 0
    d_both = jnp.where(fwd_mask, jnp.exp(jnp.where(fwd_mask, rel, 0.0) * lg_f1),
                       jnp.exp(jnp.where(fwd_mask, 0.0, -rel) * lg_b1))
    r_v = lax.broadcasted_iota(jnp.int32, (c, dv), 0).astype(F32)
    r_k = lax.broadcasted_iota(jnp.int32, (c, dk), 0).astype(F32)
    qd = jnp.concatenate([jnp.exp((r_v + 1.0) * lg_f), jnp.exp((c - r_v) * lg_b)], axis=1)
    kd = jnp.concatenate([jnp.exp((c - 1.0 - r_k) * lg_f1), jnp.exp(r_k * lg_b1)], axis=1)
    cd_f = jnp.exp(c * lg_f)
    cd_b = jnp.exp(c * lg_b)

    def chunk_rows(i):
        return pl.ds(pl.multiple_of(i * c, c), c)

    def loop_a(i, carry):
        rows = chunk_rows(i)
        k_blk = k_ref[rows, :].astype(F32)
        kdec = (jnp.concatenate([k_blk, k_blk], axis=1) * kd).T.astype(BF16)
        u_sc[i] = jnp.dot(kdec, v_ref[rows, :], preferred_element_type=F32)
        return carry

    lax.fori_loop(0, n_chunks, loop_a, 0, unroll=RET_UNROLL)

    r_m = lax.broadcasted_iota(jnp.int32, (CHUNK, dk), 0).astype(F32)
    mk_dec = (mk_ref[...].astype(F32) * jnp.exp((CHUNK - 1.0 - r_m) * lg_f1)).T.astype(BF16)
    state0 = jnp.dot(mk_dec, mv_ref[...], preferred_element_type=F32)

    def loop_b(i, carry):
        st_f, st_b = carry
        j = n_chunks - 1 - i
        st_sc[i, :, pl.ds(0, dv)] = st_f.astype(BF16)
        st_sc[j, :, pl.ds(dv, dv)] = st_b.astype(BF16)
        return st_f * cd_f + u_sc[i, pl.ds(0, dk), :], st_b * cd_b + u_sc[j, pl.ds(dk, dk), :]

    lax.fori_loop(0, n_chunks, loop_b, (state0, jnp.zeros((dk, dv), F32)), unroll=RET_UNROLL)

    gain = gain_ref[0]

    def loop_c(i, carry):
        rows = chunk_rows(i)
        q_blk, v_blk = q_ref[rows, :], v_ref[rows, :]
        s = lax.dot_general(q_blk, k_ref[rows, :], nt_dims, preferred_element_type=F32) * d_both
        intra = jnp.dot(s.astype(BF16), v_blk, preferred_element_type=F32)
        inter = jnp.dot(q_blk, st_sc[i], preferred_element_type=F32) * qd
        ro = intra + (inter[:, :dv] + inter[:, dv:])
        ms = jnp.mean(ro * ro, axis=-1, keepdims=True)
        y = (ro * lax.rsqrt(ms + NORM_EPS)) * gain
        o_ref[rows, :] = (y * _silu(g_ref[rows, :].astype(F32))).astype(o_ref.dtype)
        return carry

    lax.fori_loop(0, n_chunks, loop_c, 0, unroll=RET_UNROLL)


def _retention(projs, metas, dec_f, dec_b, ret_gain, batch, seq):
    t = batch * seq
    kb, vb = R_QK_DIM, R_V_DIM
    _, proj_p, proj_r = projs
    _, meta_p, meta_r = metas
    q_spec = pl.BlockSpec((seq, kb), lambda b, h: (b, h))
    k_spec = pl.BlockSpec((seq, kb), lambda b, h: (b, R_HEADS + h))
    v_spec = pl.BlockSpec((seq, vb), lambda b, h: (b, 2 * A_HEADS + h))
    g_spec = pl.BlockSpec((seq, vb), lambda b, h: (b, 2 * A_HEADS + R_HEADS + h))
    mk_spec = pl.BlockSpec((CHUNK, kb), lambda b, h: (0, R_HEADS + h))
    mv_spec = pl.BlockSpec((CHUNK, vb), lambda b, h: (0, 2 * A_HEADS + h))
    head_spec = pl.BlockSpec((1, 1, vb), lambda b, h: (h, 0, 0))
    n_chunks = seq // RET_CHUNK
    vmem = 2 * seq * (2 * kb + 3 * vb) * 2 + n_chunks * 2 * kb * vb * (4 + 2) + 64 * RET_CHUNK * vb * 4
    return pl.pallas_call(
        _ret_kernel,
        grid=(batch, R_HEADS),
        in_specs=[q_spec, k_spec, v_spec, g_spec, mk_spec, mv_spec, head_spec, head_spec, head_spec],
        out_specs=pl.BlockSpec((seq, vb), lambda b, h: (b, h)),
        out_shape=jax.ShapeDtypeStruct((t, R_HEADS * R_V_DIM), BF16),
        scratch_shapes=[pltpu.VMEM((n_chunks, 2 * kb, vb), F32), pltpu.VMEM((n_chunks, kb, 2 * vb), BF16)],
        compiler_params=_params(("arbitrary", "arbitrary"), vmem),
        name="retention",
    )(proj_r, proj_r, proj_p, proj_p, meta_r, meta_p, dec_f, dec_b, ret_gain)


def _out_kernel(a_ref, r_ref, w_ref, x_ref, g_ref, o_ref):
    half = a_ref.shape[1]
    h = x_ref[...] + jnp.dot(a_ref[...], w_ref[pl.ds(0, half), :], preferred_element_type=F32)
    h = h + jnp.dot(r_ref[...], w_ref[pl.ds(half, half), :], preferred_element_type=F32)
    ms = jnp.mean(h * h, axis=-1, keepdims=True)
    o_ref[...] = (h * lax.rsqrt(ms + NORM_EPS)) * g_ref[...]


def _out_proj(ao, ro, w_bf16, x2d, gain, tm):
    t, d = x2d.shape
    tm = min(tm, t)
    half = d // 2
    vmem = d * d * 2 + 2 * 2 * tm * half * 2 + 2 * 2 * tm * d * 4 + 2 * tm * d * 4
    return pl.pallas_call(
        _out_kernel,
        grid=(t // tm,),
        in_specs=[
            pl.BlockSpec((tm, half), lambda i: (i, 0)),
            pl.BlockSpec((tm, half), lambda i: (i, 0)),
            pl.BlockSpec((d, d), lambda i: (0, 0), pipeline_mode=pl.Buffered(1)),
            pl.BlockSpec((tm, d), lambda i: (i, 0)),
            pl.BlockSpec((1, d), lambda i: (0, 0)),
        ],
        out_specs=pl.BlockSpec((tm, d), lambda i: (i, 0)),
        out_shape=jax.ShapeDtypeStruct((t, d), F32),
        compiler_params=_params(("arbitrary",), vmem),
        name="out_proj",
    )(ao, ro, w_bf16, x2d, gain.reshape(1, d))


def _trunk(x, shared, u=None, side_x=None):
    batch, seq, d = x.shape
    x2d = x.reshape(batch * seq, d)
    if u is None:
        u = _rmsnorm(x2d, shared["norm_gain"], NORM_TM)
    side = None if side_x is None else (side_x, shared["norm_gain"])
    projs, u_side = _project(u, shared["w_in"], _rotary_tables(jnp.arange(seq, dtype=jnp.int32) + N_META), seq, side)
    ao = _attention(projs, shared["proj_meta"], shared["mbias"], shared["lams"], shared["subln_gain"],
                    batch, seq, ATTN_TQ, ATTN_TK)
    ro = _retention(projs, shared["proj_meta"], shared["dec_f"], shared["dec_b"], shared["ret_gain"], batch, seq)
    y = _out_proj(ao, ro, shared["w_out"], x2d, shared["final_gain"], OUT_TM)
    return y.reshape(batch, seq, d), u_side


def kernel(x_prompt, x_sample, meta_tokens, norm_gain, w_in, lambda_q1, lambda_k1, lambda_q2, lambda_k2,
           attn_subln_gain, ret_decay_fwd, ret_decay_bwd, ret_norm_gain, w_out, final_norm_gain):
    assert norm_gain.shape[0] == 1, "single-layer trunk"
    d = x_prompt.shape[-1]
    assert d == D_MODEL and w_in.shape[1:] == (D_MODEL, PROJ_OUT) and w_out.shape[1:] == (D_MODEL, D_MODEL)
    w_in_bf16 = w_in[0].astype(BF16)
    w_out_bf16 = w_out[0].astype(BF16)

    meta_pos = jnp.arange(CHUNK, dtype=jnp.int32) - (CHUNK - N_META)
    meta_rows = jnp.concatenate([jnp.zeros((CHUNK - N_META, d), F32), meta_tokens.astype(F32)], axis=0)
    u_meta = _rmsnorm(meta_rows, norm_gain[0], CHUNK)
    proj_meta, _ = _project(u_meta, w_in_bf16, _rotary_tables(meta_pos), CHUNK)
    mbias = jnp.where(meta_pos >= 0, 0.0, NEG_BIG).astype(F32).reshape(1, CHUNK)

    def per_head(v):
        return jnp.broadcast_to(v.astype(F32)[:, None, None], (R_HEADS, 1, R_V_DIM))

    shared = dict(
        norm_gain=norm_gain[0], w_in=w_in_bf16, w_out=w_out_bf16, proj_meta=proj_meta, mbias=mbias,
        lams=tuple(v[0].astype(F32).reshape(1, A_QK_DIM) for v in (lambda_q1, lambda_k1, lambda_q2, lambda_k2)),
        subln_gain=attn_subln_gain[0].astype(F32),
        dec_f=per_head(ret_decay_fwd[0]), dec_b=per_head(ret_decay_bwd[0]),
        ret_gain=ret_norm_gain[0].astype(F32).reshape(R_HEADS, 1, R_V_DIM),
        final_gain=final_norm_gain.astype(F32),
    )
    xs2d = x_sample.reshape(-1, d)
    side_x = xs2d if _side_rmsnorm_fits(x_prompt.shape[0] * x_prompt.shape[1], xs2d.shape[0]) else None
    y_prompt, u_sample = _trunk(x_prompt, shared, side_x=side_x)
    y_sample, _ = _trunk(x_sample, shared, u=u_sample)
    return (y_prompt, y_sample)
```

```python
import functools
import math

import jax
import jax.numpy as jnp
from jax import lax
from jax.experimental import pallas as pl
from jax.experimental.pallas import tpu as pltpu

F32 = jnp.float32
BF16 = jnp.bfloat16

D_MODEL = 4096
N_META = 16
CHUNK = 128
A_HEADS = 8
A_QK_DIM = 128
A_V_DIM = 256
A_ROT_DIM = A_QK_DIM // 4
ROPE_THETA = 500000.0
R_HEADS = 8
R_QK_DIM = 128
R_V_DIM = 256
RET_THETA = 10000.0
NORM_EPS = 1e-6
NEG_BIG = -1e30
LAMBDA_INIT = 0.8 - 0.6 * math.exp(-0.3 * 0)

AQ_OFF = 0
AV_OFF = AQ_OFF + 2 * (2 * A_HEADS * A_QK_DIM)
RQ_OFF = AV_OFF + 2 * (A_HEADS * A_V_DIM)
RV_OFF = RQ_OFF + 2 * (R_HEADS * R_QK_DIM)
PROJ_OUT = RV_OFF + 2 * (R_HEADS * R_V_DIM)

LANES = 128
BF16_SUBLANES = 16
LOG2E = math.log2(math.e)
F32_MAX = float(jnp.finfo(jnp.float32).max)
ATTN_SCORE_SCALE = (A_QK_DIM ** -0.5) * LOG2E
V7X_VMEM_BYTES = 64 * 1024 * 1024
V7X_VMEM_BUDGET = V7X_VMEM_BYTES - 4 * 1024 * 1024

PROJ_TM = 1024
PROJ_TN = 1024
PROJ_RM = 256
ATTN_TQ = 1024
ATTN_TK = 512
ATTN_RT = 128
ATTN_GUARD = 2.0 ** 64
ATTN_UNROLL = 5
ATTN_FAST_UNROLL = 15
OUT_TM = 256
RET_CHUNK = 256
RET_UNROLL = 4
NORM_TM = 512


def _params(sem, vmem_bytes):
    return pltpu.CompilerParams(
        dimension_semantics=sem, vmem_limit_bytes=min(int(vmem_bytes), V7X_VMEM_BUDGET)
    )


def _rmsnorm_kernel(x_ref, g_ref, o_ref):
    x = x_ref[...]
    ms = jnp.mean(x * x, axis=-1, keepdims=True)
    y = x * lax.rsqrt(ms + NORM_EPS)
    o_ref[...] = (y * g_ref[...]).astype(o_ref.dtype)


def _rmsnorm(x2d, gain, tm):
    t, d = x2d.shape
    tm = min(tm, t)
    return pl.pallas_call(
        _rmsnorm_kernel,
        grid=(t // tm,),
        in_specs=[pl.BlockSpec((tm, d), lambda i: (i, 0)), pl.BlockSpec((1, d), lambda i: (0, 0))],
        out_specs=pl.BlockSpec((tm, d), lambda i: (i, 0)),
        out_shape=jax.ShapeDtypeStruct((t, d), BF16),
        compiler_params=_params(("arbitrary",), 2 * tm * d * (4 + 2) + 4 * tm * d * 4),
        name="rmsnorm_in",
    )(x2d, gain.reshape(1, d))


def _proj_plain_kernel(u_ref, w_ref, o_ref, *, rm):
    for c in range(u_ref.shape[0] // rm):
        rows = pl.ds(c * rm, rm)
        o_ref[rows, :] = jnp.dot(u_ref[rows, :], w_ref[...], preferred_element_type=F32).astype(o_ref.dtype)


def _proj_plain_side_kernel(u_ref, w_ref, xs_ref, gs_ref, o_ref, us_ref, *, rm):
    _proj_plain_kernel(u_ref, w_ref, o_ref, rm=rm)
    _rmsnorm_kernel(xs_ref, gs_ref, us_ref)


def _proj_rot_kernel(u_ref, w_ref, cos_ref, sin1_ref, sin2_ref, mult_ref, o_ref, *, rm, shift1, shift2):
    for c in range(u_ref.shape[0] // rm):
        rows = pl.ds(c * rm, rm)
        acc = jnp.dot(u_ref[rows, :], w_ref[...], preferred_element_type=F32)
        cos, sin1, sin2, mult = cos_ref[rows, :], sin1_ref[rows, :], sin2_ref[rows, :], mult_ref[rows, :]
        for g in range(acc.shape[1] // LANES):
            x = acc[:, g * LANES:(g + 1) * LANES]
            y = x * cos + pltpu.roll(x, shift1, 1) * sin1 + pltpu.roll(x, shift2, 1) * sin2
            o_ref[rows, g * LANES:(g + 1) * LANES] = (y * mult).astype(o_ref.dtype)


def _proj_plain(u, w_bf16, tm, tn, rm, side=None):
    t, d = u.shape
    tm = min(tm, t)
    n_a = (RQ_OFF - AV_OFF) // tn
    n_cols = (RQ_OFF - AV_OFF) + (PROJ_OUT - RV_OFF)
    vmem = 2 * tm * d * 2 + 2 * d * tn * 2 + 2 * tm * tn * 2 + 2 * min(rm, tm) * tn * 4
    n_j = n_cols // tn
    in_specs = [pl.BlockSpec((tm, d), lambda i, j: (i, 0)),
                pl.BlockSpec((d, tn), lambda i, j: (0, jnp.where(j < n_a, AV_OFF // tn + j, RV_OFF // tn + j - n_a)))]
    out_specs = pl.BlockSpec((tm, tn), lambda i, j: (i, j))
    out_shape = jax.ShapeDtypeStruct((t, n_cols), BF16)
    args = (u, w_bf16)
    body = _proj_plain_kernel
    if side is not None:
        body = _proj_plain_side_kernel
        xs, gain = side
        rs = xs.shape[0] // ((t // tm) * n_j)
        side_spec = pl.BlockSpec((rs, d), lambda i, j: (i * n_j + j, 0))
        in_specs += [side_spec, pl.BlockSpec((1, d), lambda i, j: (0, 0))]
        out_specs = (out_specs, side_spec)
        out_shape = (out_shape, jax.ShapeDtypeStruct(xs.shape, BF16))
        args += (xs, gain.reshape(1, d))
        vmem += 2 * rs * d * (4 + 2) + 4 * rs * d * 4
    return pl.pallas_call(
        functools.partial(body, rm=min(rm, tm)),
        grid=(t // tm, n_j),
        in_specs=in_specs,
        out_specs=out_specs,
        out_shape=out_shape,
        compiler_params=_params(("arbitrary", "arbitrary"), vmem),
        name="in_proj_plain",
    )(*args)


def _proj_rot_cast_kernel(u_ref, w_ref, cos_ref, sin1_ref, sin2_ref, mult_ref, ws_ref, o_ref, wb_ref, *, rm,
                          shift1, shift2):
    _proj_rot_kernel(u_ref, w_ref, cos_ref, sin1_ref, sin2_ref, mult_ref, o_ref, rm=rm, shift1=shift1, shift2=shift2)
    wb_ref[...] = ws_ref[...].astype(wb_ref.dtype)


def _proj_rot(u, w_bf16, tables, mults, col_off, n_cols, shifts, rows_per_seq, tm, tn, rm, name, cast_side=None):
    t, d = u.shape
    tm = min(tm, rows_per_seq)
    n_seq_tiles = rows_per_seq // tm
    n_j = n_cols // tn
    tab_spec = pl.BlockSpec((tm, LANES), lambda i, j: (i % n_seq_tiles, 0))
    mult_spec = pl.BlockSpec((None, tm, LANES), lambda i, j: (j // (n_j // 2), i % n_seq_tiles, 0))
    vmem = 2 * tm * d * 2 + 2 * d * tn * 2 + 2 * tm * tn * 2 + 2 * 4 * tm * LANES * 4 + 3 * min(rm, tm) * tn * 4
    body = _proj_rot_kernel
    in_specs = [pl.BlockSpec((tm, d), lambda i, j: (i, 0)),
                pl.BlockSpec((d, tn), lambda i, j: (0, col_off // tn + j)),
                tab_spec, tab_spec, tab_spec, mult_spec]
    out_specs = pl.BlockSpec((tm, tn), lambda i, j: (i, j))
    out_shape = jax.ShapeDtypeStruct((t, n_cols), BF16)
    args = (u, w_bf16, *tables, mults)
    if cast_side is not None:
        body = _proj_rot_cast_kernel
        rs = cast_side.shape[0] // ((t // tm) * n_j)
        side_spec = pl.BlockSpec((rs, cast_side.shape[1]), lambda i, j: (i * n_j + j, 0))
        in_specs += [side_spec]
        out_specs = (out_specs, side_spec)
        out_shape = (out_shape, jax.ShapeDtypeStruct(cast_side.shape, BF16))
        args += (cast_side,)
        vmem += 2 * rs * cast_side.shape[1] * (4 + 2)
    return pl.pallas_call(
        functools.partial(body, rm=min(rm, tm), shift1=shifts[0], shift2=shifts[1]),
        grid=(t // tm, n_j),
        in_specs=in_specs,
        out_specs=out_specs,
        out_shape=out_shape,
        compiler_params=_params(("arbitrary", "arbitrary"), vmem),
        name=name,
    )(*args)


def _side_rmsnorm_fits(t_main, t_side):
    steps = (t_main // min(PROJ_TM, t_main)) * ((RQ_OFF - AV_OFF + PROJ_OUT - RV_OFF) // PROJ_TN)
    return t_side % steps == 0 and (t_side // steps) % BF16_SUBLANES == 0


def _side_cast_fits(t_main, rows_per_seq, n_rows):
    steps = (t_main // min(PROJ_TM, rows_per_seq)) * ((AV_OFF - AQ_OFF) // PROJ_TN)
    return n_rows % steps == 0 and (n_rows // steps) % BF16_SUBLANES == 0


def _project(u, w_bf16, tabs, rows_per_seq, side=None, cast_side=None):
    proj_a = _proj_rot(u, w_bf16, tabs["attn"], tabs["attn_mult"], AQ_OFF, AV_OFF - AQ_OFF,
                       (LANES - A_ROT_DIM // 2, A_ROT_DIM // 2), rows_per_seq, PROJ_TM, PROJ_TN, PROJ_RM, "in_proj_attn",
                       cast_side)
    w_cast = None
    if cast_side is not None:
        proj_a, w_cast = proj_a
    proj_p = _proj_plain(u, w_bf16, PROJ_TM, PROJ_TN, PROJ_RM, side)
    u_side = None
    if side is not None:
        proj_p, u_side = proj_p
    proj_r = _proj_rot(u, w_bf16, tabs["ret"], tabs["ret_mult"], RQ_OFF, RV_OFF - RQ_OFF,
                       (R_QK_DIM // 2, R_QK_DIM // 2), rows_per_seq, PROJ_TM, PROJ_TN, PROJ_RM, "in_proj_ret")
    return (proj_a, proj_p, proj_r), u_side, w_cast


def _rotary_tables(pos):
    n = pos.shape[0]
    posf = pos.astype(F32)[:, None]
    half = A_ROT_DIM // 2
    inv = jnp.power(jnp.float32(ROPE_THETA), -jnp.arange(half, dtype=F32) * 2.0 / A_ROT_DIM)
    ang = posf * inv[None, :]
    cos, sin = jnp.cos(ang), jnp.sin(ang)
    ones = jnp.ones((n, LANES - A_ROT_DIM), F32)
    zeros_h = jnp.zeros((n, half), F32)
    zeros_r = jnp.zeros((n, LANES - A_ROT_DIM), F32)
    cos_a = jnp.concatenate([cos, cos, ones], axis=1)
    sin_hi = jnp.concatenate([-sin, zeros_h, zeros_r], axis=1)
    sin_lo = jnp.concatenate([zeros_h, sin, zeros_r], axis=1)
    half_r = R_QK_DIM // 2
    inv_r = jnp.power(jnp.float32(RET_THETA), -jnp.arange(half_r, dtype=F32) * 2.0 / R_QK_DIM)
    ang_r = posf * inv_r[None, :]
    cos_r = jnp.concatenate([jnp.cos(ang_r), jnp.cos(ang_r)], axis=1)
    sin_r = jnp.concatenate([-jnp.sin(ang_r), jnp.sin(ang_r)], axis=1)
    one = jnp.ones((n, LANES), F32)
    kmask = jnp.broadcast_to((pos >= 0).astype(F32)[:, None], (n, LANES))
    return dict(
        attn=(cos_a, sin_hi, sin_lo),
        attn_mult=jnp.stack([one * ATTN_SCORE_SCALE, one]),
        ret=(cos_r, sin_r, jnp.zeros((n, LANES), F32)),
        ret_mult=jnp.stack([one, (R_QK_DIM ** -0.5) * kmask]),
    )


def _silu(g):
    h = 0.5 * g
    return h + h * jnp.tanh(h)


def _attn_kernel(q_ref, k_ref, v_ref, g_ref, mk_ref, mv_ref, mbias_ref, lq1_ref, lk1_ref, lq2_ref, lk2_ref,
                 gain_ref, o_ref, m_sc, l_sc, acc_sc, *, tk, rt):
    tq = q_ref.shape[0]
    n_kb = k_ref.shape[0] // tk
    nt_dims = (((1,), (1,)), ((), ()))
    sub = (pl.ds(0, A_QK_DIM), pl.ds(A_QK_DIM, A_QK_DIM))

    def lane_tiles(x):
        return [x[:, t * LANES:(t + 1) * LANES] for t in range(x.shape[1] // LANES)]

    def update(idx, rows, s, v_blk, first):
        tiles = lane_tiles(s)
        mx = functools.reduce(jnp.maximum, tiles)
        m_cur = jnp.max(mx, axis=-1, keepdims=True)
        if first:
            m_new = jnp.broadcast_to(m_cur, (rt, LANES))
        else:
            m_prev = m_sc[idx, rows, :]
            m_new = jnp.maximum(m_prev, m_cur)
        p_tiles = [jnp.exp2(t - m_new) for t in tiles]
        p = jnp.concatenate(p_tiles, axis=1) if len(p_tiles) > 1 else p_tiles[0]
        pv = jnp.dot(p.astype(BF16), v_blk, preferred_element_type=F32)
        l_part = functools.reduce(jnp.add, p_tiles)
        if first:
            l_sc[idx, rows, :] = l_part
            acc_sc[idx, rows, :] = pv
        else:
            alpha = jnp.exp2(m_prev - m_new)
            l_sc[idx, rows, :] = alpha * l_sc[idx, rows, :] + l_part
            acc_sc[idx, rows, :] = jnp.concatenate([alpha, alpha], axis=1) * acc_sc[idx, rows, :] + pv
        m_sc[idx, rows, :] = m_new

    def scores(rows, idx, keys):
        return lax.dot_general(q_ref[rows, sub[idx]], keys, nt_dims, preferred_element_type=F32)

    def update_fixed(idx, rows, s, v_blk):
        tiles = lane_tiles(s)
        m_ref = m_sc[idx, rows, :]
        p_tiles = [jnp.exp2(t - m_ref) for t in tiles]
        p = jnp.concatenate(p_tiles, axis=1)
        acc_sc[idx, rows, :] += jnp.dot(p.astype(BF16), v_blk, preferred_element_type=F32)
        l_sc[idx, rows, :] += functools.reduce(jnp.add, p_tiles)

    def first_step():
        mbias = mbias_ref[...] * LOG2E
        v_first = jnp.concatenate([mv_ref[...], v_ref[pl.ds(0, tk), :]], axis=0)
        for r in range(tq // rt):
            rows = pl.ds(r * rt, rt)
            for idx in range(2):
                s_meta = scores(rows, idx, mk_ref[:, sub[idx]]) + mbias
                s_tok = scores(rows, idx, k_ref[pl.ds(0, tk), sub[idx]])
                update(idx, rows, jnp.concatenate([s_meta, s_tok], axis=1), v_first, True)

    def kv_loop(step, unroll):
        def body(kb, carry):
            off = pl.multiple_of(kb * tk, tk)
            v_blk = v_ref[pl.ds(off, tk), :]
            for r in range(tq // rt):
                rows = pl.ds(r * rt, rt)
                for idx in range(2):
                    step(idx, rows, scores(rows, idx, k_ref[pl.ds(off, tk), sub[idx]]), v_blk)
            return carry

        lax.fori_loop(1, n_kb, body, 0, unroll=unroll)

    def finish():
        lam = (jnp.exp(jnp.sum(lq1_ref[...] * lk1_ref[...], axis=-1, keepdims=True))
               - jnp.exp(jnp.sum(lq2_ref[...] * lk2_ref[...], axis=-1, keepdims=True)) + LAMBDA_INIT)
        inv1 = 1.0 / jnp.sum(l_sc[0], axis=-1, keepdims=True)
        inv2 = lam / jnp.sum(l_sc[1], axis=-1, keepdims=True)
        o = acc_sc[0] * inv1 - acc_sc[1] * inv2
        ms = jnp.mean(o * o, axis=-1, keepdims=True)
        y = (o * lax.rsqrt(ms + NORM_EPS)) * (gain_ref[...] * (1.0 - LAMBDA_INIT))
        o_ref[...] = (y * _silu(g_ref[...].astype(F32))).astype(o_ref.dtype)
        return ms

    first_step()
    kv_loop(update_fixed, ATTN_FAST_UNROLL)
    ms = finish()
    in_range = jnp.minimum(jnp.min(jnp.where(l_sc[...] < ATTN_GUARD, 1.0, 0.0)),
                           jnp.min(jnp.where(ms < F32_MAX, 1.0, 0.0)))

    @pl.when(in_range < 0.5)
    def _():
        first_step()
        kv_loop(functools.partial(update, first=False), ATTN_UNROLL)
        finish()


def _attention(projs, metas, mbias, lams, subln_gain, batch, seq, tq, tk):
    t = batch * seq
    tq, tk = min(tq, seq), min(tk, seq)
    nq = seq // tq
    hb = A_V_DIM
    proj_a, proj_p, _ = projs
    meta_a, meta_p, _ = metas
    q_spec = pl.BlockSpec((tq, hb), lambda b, h, i: (b * nq + i, h))
    k_spec = pl.BlockSpec((seq, hb), lambda b, h, i: (b, A_HEADS + h))
    v_spec = pl.BlockSpec((seq, hb), lambda b, h, i: (b, h))
    g_spec = pl.BlockSpec((tq, hb), lambda b, h, i: (b * nq + i, A_HEADS + h))
    mk_spec = pl.BlockSpec((CHUNK, hb), lambda b, h, i: (0, A_HEADS + h))
    mv_spec = pl.BlockSpec((CHUNK, hb), lambda b, h, i: (0, h))
    row128 = pl.BlockSpec((1, LANES), lambda b, h, i: (0, 0))
    gain_spec = pl.BlockSpec((1, A_V_DIM), lambda b, h, i: (0, 0))
    vmem = (2 * 2 * seq * hb * 2 + 2 * 3 * tq * hb * 2 + 2 * 2 * CHUNK * hb * 2
            + 2 * tq * (2 * LANES + A_V_DIM) * 4 + 6 * tq * tk * 4)
    return pl.pallas_call(
        functools.partial(_attn_kernel, tk=tk, rt=min(ATTN_RT, tq)),
        grid=(batch, A_HEADS, nq),
        in_specs=[q_spec, k_spec, v_spec, g_spec, mk_spec, mv_spec, row128, row128, row128, row128, row128,
                  gain_spec],
        out_specs=pl.BlockSpec((tq, hb), lambda b, h, i: (b * nq + i, h)),
        out_shape=jax.ShapeDtypeStruct((t, A_HEADS * A_V_DIM), BF16),
        scratch_shapes=[pltpu.VMEM((2, tq, LANES), F32), pltpu.VMEM((2, tq, LANES), F32),
                        pltpu.VMEM((2, tq, A_V_DIM), F32)],
        compiler_params=_params(("arbitrary", "arbitrary", "arbitrary"), vmem),
        name="diff_attention",
    )(proj_a, proj_a, proj_p, proj_p, meta_a, meta_p, mbias, *lams, subln_gain.reshape(1, A_V_DIM))


def _log_sigmoid(x):
    return jnp.minimum(x, 0.0) - jnp.log(1.0 + jnp.exp(-jnp.abs(x)))


def _ret_kernel(q_ref, k_ref, v_ref, g_ref, mk_ref, mv_ref, decf_ref, decb_ref, gain_ref, o_ref, u_sc, st_sc):
    c = RET_CHUNK
    dk, dv = R_QK_DIM, R_V_DIM
    n_chunks = q_ref.shape[0] // c
    nt_dims = (((1,), (1,)), ((), ()))
    lg_f = _log_sigmoid(decf_ref[0])
    lg_b = _log_sigmoid(decb_ref[0])
    lg_f1, lg_b1 = lg_f[:, :1], lg_b[:, :1]

    row = lax.broadcasted_iota(jnp.int32, (c, c), 0)
    col = lax.broadcasted_iota(jnp.int32, (c, c), 1)
    rel = (row - col).astype(F32)
    fwd_mask = rel >= 0
    d_both = jnp.where(fwd_mask, jnp.exp(jnp.where(fwd_mask, rel, 0.0) * lg_f1),
                       jnp.exp(jnp.where(fwd_mask, 0.0, -rel) * lg_b1))
    r_v = lax.broadcasted_iota(jnp.int32, (c, dv), 0).astype(F32)
    r_k = lax.broadcasted_iota(jnp.int32, (c, dk), 0).astype(F32)
    qd = jnp.concatenate([jnp.exp((r_v + 1.0) * lg_f), jnp.exp((c - r_v) * lg_b)], axis=1)
    kd = jnp.concatenate([jnp.exp((c - 1.0 - r_k) * lg_f1), jnp.exp(r_k * lg_b1)], axis=1)
    cd_f = jnp.exp(c * lg_f)
    cd_b = jnp.exp(c * lg_b)

    def chunk_rows(i):
        return pl.ds(pl.multiple_of(i * c, c), c)

    def loop_a(i, carry):
        rows = chunk_rows(i)
        k_blk = k_ref[rows, :].astype(F32)
        kdec = (jnp.concatenate([k_blk, k_blk], axis=1) * kd).T.astype(BF16)
        u_sc[i] = jnp.dot(kdec, v_ref[rows, :], preferred_element_type=F32)
        return carry

    lax.fori_loop(0, n_chunks, loop_a, 0, unroll=RET_UNROLL)

    r_m = lax.broadcasted_iota(jnp.int32, (CHUNK, dk), 0).astype(F32)
    mk_dec = (mk_ref[...].astype(F32) * jnp.exp((CHUNK - 1.0 - r_m) * lg_f1)).T.astype(BF16)
    state0 = jnp.dot(mk_dec, mv_ref[...], preferred_element_type=F32)

    def loop_b(i, carry):
        st_f, st_b = carry
        j = n_chunks - 1 - i
        st_sc[i, :, pl.ds(0, dv)] = st_f.astype(BF16)
        st_sc[j, :, pl.ds(dv, dv)] = st_b.astype(BF16)
        return st_f * cd_f + u_sc[i, pl.ds(0, dk), :], st_b * cd_b + u_sc[j, pl.ds(dk, dk), :]

    lax.fori_loop(0, n_chunks, loop_b, (state0, jnp.zeros((dk, dv), F32)), unroll=RET_UNROLL)

    gain = gain_ref[0]

    def loop_c(i, carry):
        rows = chunk_rows(i)
        q_blk, v_blk = q_ref[rows, :], v_ref[rows, :]
        s = lax.dot_general(q_blk, k_ref[rows, :], nt_dims, preferred_element_type=F32) * d_both
        intra = jnp.dot(s.astype(BF16), v_blk, preferred_element_type=F32)
        inter = jnp.dot(q_blk, st_sc[i], preferred_element_type=F32) * qd
        ro = intra + (inter[:, :dv] + inter[:, dv:])
        ms = jnp.mean(ro * ro, axis=-1, keepdims=True)
        y = (ro * lax.rsqrt(ms + NORM_EPS)) * gain
        o_ref[rows, :] = (y * _silu(g_ref[rows, :].astype(F32))).astype(o_ref.dtype)
        return carry

    lax.fori_loop(0, n_chunks, loop_c, 0, unroll=RET_UNROLL)


def _retention(projs, metas, dec_f, dec_b, ret_gain, batch, seq):
    t = batch * seq
    kb, vb = R_QK_DIM, R_V_DIM
    _, proj_p, proj_r = projs
    _, meta_p, meta_r = metas
    q_spec = pl.BlockSpec((seq, kb), lambda b, h: (b, h))
    k_spec = pl.BlockSpec((seq, kb), lambda b, h: (b, R_HEADS + h))
    v_spec = pl.BlockSpec((seq, vb), lambda b, h: (b, 2 * A_HEADS + h))
    g_spec = pl.BlockSpec((seq, vb), lambda b, h: (b, 2 * A_HEADS + R_HEADS + h))
    mk_spec = pl.BlockSpec((CHUNK, kb), lambda b, h: (0, R_HEADS + h))
    mv_spec = pl.BlockSpec((CHUNK, vb), lambda b, h: (0, 2 * A_HEADS + h))
    head_spec = pl.BlockSpec((1, 1, vb), lambda b, h: (h, 0, 0))
    n_chunks = seq // RET_CHUNK
    vmem = 2 * seq * (2 * kb + 3 * vb) * 2 + n_chunks * 2 * kb * vb * (4 + 2) + 64 * RET_CHUNK * vb * 4
    return pl.pallas_call(
        _ret_kernel,
        grid=(batch, R_HEADS),
        in_specs=[q_spec, k_spec, v_spec, g_spec, mk_spec, mv_spec, head_spec, head_spec, head_spec],
        out_specs=pl.BlockSpec((seq, vb), lambda b, h: (b, h)),
        out_shape=jax.ShapeDtypeStruct((t, R_HEADS * R_V_DIM), BF16),
        scratch_shapes=[pltpu.VMEM((n_chunks, 2 * kb, vb), F32), pltpu.VMEM((n_chunks, kb, 2 * vb), BF16)],
        compiler_params=_params(("arbitrary", "arbitrary"), vmem),
        name="retention",
    )(proj_r, proj_r, proj_p, proj_p, meta_r, meta_p, dec_f, dec_b, ret_gain)


def _out_kernel(a_ref, r_ref, w_ref, x_ref, g_ref, o_ref):
    half = a_ref.shape[1]
    h = x_ref[...] + jnp.dot(a_ref[...], w_ref[pl.ds(0, half), :], preferred_element_type=F32)
    h = h + jnp.dot(r_ref[...], w_ref[pl.ds(half, half), :], preferred_element_type=F32)
    ms = jnp.mean(h * h, axis=-1, keepdims=True)
    o_ref[...] = (h * lax.rsqrt(ms + NORM_EPS)) * g_ref[...]


def _out_proj(ao, ro, w_bf16, x2d, gain, tm):
    t, d = x2d.shape
    tm = min(tm, t)
    half = d // 2
    vmem = d * d * 2 + 2 * 2 * tm * half * 2 + 2 * 2 * tm * d * 4 + 2 * tm * d * 4
    return pl.pallas_call(
        _out_kernel,
        grid=(t // tm,),
        in_specs=[
            pl.BlockSpec((tm, half), lambda i: (i, 0)),
            pl.BlockSpec((tm, half), lambda i: (i, 0)),
            pl.BlockSpec((d, d), lambda i: (0, 0), pipeline_mode=pl.Buffered(1)),
            pl.BlockSpec((tm, d), lambda i: (i, 0)),
            pl.BlockSpec((1, d), lambda i: (0, 0)),
        ],
        out_specs=pl.BlockSpec((tm, d), lambda i: (i, 0)),
        out_shape=jax.ShapeDtypeStruct((t, d), F32),
        compiler_params=_params(("arbitrary",), vmem),
        name="out_proj",
    )(ao, ro, w_bf16, x2d, gain.reshape(1, d))


def _trunk(x, shared, u=None, side_x=None):
    batch, seq, d = x.shape
    x2d = x.reshape(batch * seq, d)
    if u is None:
        u = _rmsnorm(x2d, shared["norm_gain"], NORM_TM)
    side = None if side_x is None else (side_x, shared["norm_gain"])
    w_out = shared["w_out"]
    cast_side = w_out if w_out.dtype == F32 and _side_cast_fits(batch * seq, seq, w_out.shape[0]) else None
    projs, u_side, w_cast = _project(u, shared["w_in"], _rotary_tables(jnp.arange(seq, dtype=jnp.int32) + N_META),
                                     seq, side, cast_side)
    w_out = w_cast if w_cast is not None else w_out.astype(BF16)
    ao = _attention(projs, shared["proj_meta"], shared["mbias"], shared["lams"], shared["subln_gain"],
                    batch, seq, ATTN_TQ, ATTN_TK)
    ro = _retention(projs, shared["proj_meta"], shared["dec_f"], shared["dec_b"], shared["ret_gain"], batch, seq)
    y = _out_proj(ao, ro, w_out, x2d, shared["final_gain"], OUT_TM)
    return y.reshape(batch, seq, d), u_side, w_out


def kernel(x_prompt, x_sample, meta_tokens, norm_gain, w_in, lambda_q1, lambda_k1, lambda_q2, lambda_k2,
           attn_subln_gain, ret_decay_fwd, ret_decay_bwd, ret_norm_gain, w_out, final_norm_gain):
    assert norm_gain.shape[0] == 1, "single-layer trunk"
    d = x_prompt.shape[-1]
    assert d == D_MODEL and w_in.shape[1:] == (D_MODEL, PROJ_OUT) and w_out.shape[1:] == (D_MODEL, D_MODEL)
    w_in_bf16 = w_in[0].astype(BF16)

    meta_pos = jnp.arange(CHUNK, dtype=jnp.int32) - (CHUNK - N_META)
    meta_rows = jnp.concatenate([jnp.zeros((CHUNK - N_META, d), F32), meta_tokens.astype(F32)], axis=0)
    u_meta = _rmsnorm(meta_rows, norm_gain[0], CHUNK)
    proj_meta, _, _ = _project(u_meta, w_in_bf16, _rotary_tables(meta_pos), CHUNK)
    mbias = jnp.where(meta_pos >= 0, 0.0, NEG_BIG).astype(F32).reshape(1, CHUNK)

    def per_head(v):
        return jnp.broadcast_to(v.astype(F32)[:, None, None], (R_HEADS, 1, R_V_DIM))

    shared = dict(
        norm_gain=norm_gain[0], w_in=w_in_bf16, w_out=w_out[0].astype(F32), proj_meta=proj_meta, mbias=mbias,
        lams=tuple(v[0].astype(F32).reshape(1, A_QK_DIM) for v in (lambda_q1, lambda_k1, lambda_q2, lambda_k2)),
        subln_gain=attn_subln_gain[0].astype(F32),
        dec_f=per_head(ret_decay_fwd[0]), dec_b=per_head(ret_decay_bwd[0]),
        ret_gain=ret_norm_gain[0].astype(F32).reshape(R_HEADS, 1, R_V_DIM),
        final_gain=final_norm_gain.astype(F32),
    )
    xs2d = x_sample.reshape(-1, d)
    side_x = xs2d if _side_rmsnorm_fits(x_prompt.shape[0] * x_prompt.shape[1], xs2d.shape[0]) else None
    y_prompt, u_sample, w_out_bf16 = _trunk(x_prompt, shared, side_x=side_x)
    y_sample, _, _ = _trunk(x_sample, dict(shared, w_out=w_out_bf16), u=u_sample)
    return (y_prompt, y_sample)
```
